```python
import math
import jax, jax.numpy as jnp
from jax import lax
import numpy as np

D_MODEL = 2048
BATCH = 4
SEQ = 8192
DEPTH = 2
DEC_BATCH = 32
DEC_SEQ = 64
PAST_LEN = 1024

CHUNK = 64
N_MIXERS = 2
POOL_WINDOWS = (2, 4, 8, 16)
N_POOL_GROUPS = len(POOL_WINDOWS)
POOL_GW = D_MODEL // N_POOL_GROUPS
POOL_STATE = max(POOL_WINDOWS) - 1
N_HEADS = 8
HEAD_HALF = D_MODEL // (2 * N_HEADS)
HEAD_DIM = 2 * HEAD_HALF
D_FF = -(-8 * D_MODEL // (3 * 256)) * 256
Q_BLOCK = 128
EPS = 1e-6
NEG_INF = -1e30

kernel_name = "hybrid_pool_diffattn_stream_step"


def rms_norm(x, g):
    xf = x.astype(jnp.float32)
    y = xf * lax.rsqrt(jnp.mean(xf * xf, axis=-1, keepdims=True) + EPS)
    return (y * g.astype(jnp.float32)).astype(x.dtype)


def alibi_slopes():
    h = jnp.arange(1, N_HEADS + 1, dtype=jnp.float32)
    return 2.0 ** (-8.0 * h / N_HEADS)


def swiglu(x, g, w_gate, w_up, w_down):
    h = rms_norm(x, g)
    return (jax.nn.silu(h @ w_gate) * (h @ w_up)) @ w_down


def pool_mixer(u, prefix, n_valid_prefix, pool_w, pool_b, pool_scale):
    B, T, _ = u.shape
    uf = u.astype(jnp.float32)
    ext = jnp.concatenate([prefix.astype(jnp.float32), uf], axis=1)
    cs = jnp.concatenate([jnp.zeros((B, 1, D_MODEL), jnp.float32),
                          jnp.cumsum(ext, axis=1)], axis=1)
    t = jnp.arange(T, dtype=jnp.int32)
    groups = []
    for g, w in enumerate(POOL_WINDOWS):
        c0, c1 = g * POOL_GW, (g + 1) * POOL_GW
        hi = cs[:, POOL_STATE + 1:POOL_STATE + 1 + T, c0:c1]
        lo = cs[:, POOL_STATE + 1 - w:POOL_STATE + 1 - w + T, c0:c1]
        cnt = jnp.minimum(w, t + 1 + n_valid_prefix).astype(jnp.float32)
        groups.append((hi - lo) / cnt[None, :, None])
    pooled = jnp.concatenate(groups, axis=-1) - uf
    mixed = jnp.einsum('btgc,gcd->btgd',
                       pooled.reshape(B, T, N_POOL_GROUPS, POOL_GW),
                       pool_w.astype(jnp.float32)).reshape(B, T, D_MODEL)
    out = (mixed + pool_b.astype(jnp.float32)) * pool_scale.astype(jnp.float32)
    new_state = ext[:, -POOL_STATE:]
    return out.astype(u.dtype), new_state.astype(u.dtype)


def diff_attend(q, k, v, q_pos, k_pos, lam):
    s = jnp.einsum('bqhcd,bkhcd->bhcqk', q.astype(jnp.float32),
                   k.astype(jnp.float32)) * (HEAD_HALF ** -0.5)
    dist = jnp.abs(q_pos[:, None] - k_pos[None, :]).astype(jnp.float32)
    bias = -alibi_slopes()[:, None, None] * dist
    visible = (k_pos[None, :] // CHUNK) <= (q_pos[:, None] // CHUNK)
    s = jnp.where(visible, s + bias[None, :, None], NEG_INF)
    p = jax.nn.softmax(s, axis=-1)
    a = p[:, :, 0] - lam * p[:, :, 1]
    return jnp.einsum('bhqk,bkhe->bqhe', a, v.astype(jnp.float32))


def diff_attn_mixer(u, past_k, past_v, lam_init, w_qkv, q_norm, k_norm,
                    lambda_q1, lambda_k1, lambda_q2, lambda_k2, subln, w_o):
    B, T, _ = u.shape
    q, k, v = jnp.split(u @ w_qkv, 3, axis=-1)
    q = rms_norm(q.reshape(B, T, N_HEADS, 2, HEAD_HALF), q_norm)
    k = rms_norm(k.reshape(B, T, N_HEADS, 2, HEAD_HALF), k_norm)
    v = v.reshape(B, T, N_HEADS, HEAD_DIM)
    f32 = jnp.float32
    lam = (jnp.exp(jnp.sum(lambda_q1.astype(f32) * lambda_k1.astype(f32)))
           - jnp.exp(jnp.sum(lambda_q2.astype(f32) * lambda_k2.astype(f32))) + lam_init)
    if past_k is None:
        n_blk = T // Q_BLOCK
        qb = q.reshape(B, n_blk, Q_BLOCK, N_HEADS, 2, HEAD_HALF).swapaxes(0, 1)
        k_pos = jnp.arange(T, dtype=jnp.int32)

        def one_block(args):
            qi, bi = args
            q_pos = bi * Q_BLOCK + jnp.arange(Q_BLOCK, dtype=jnp.int32)
            return diff_attend(qi, k, v, q_pos, k_pos, lam)

        o = lax.map(one_block, (qb, jnp.arange(n_blk, dtype=jnp.int32)))
        o = o.swapaxes(0, 1).reshape(B, T, N_HEADS, HEAD_DIM)
    else:
        P = past_k.shape[1]
        kk = jnp.concatenate([past_k.reshape(B, P, N_HEADS, 2, HEAD_HALF).astype(k.dtype), k], axis=1)
        vv = jnp.concatenate([past_v.astype(v.dtype), v], axis=1)
        q_pos = P + jnp.arange(T, dtype=jnp.int32)
        k_pos = jnp.arange(P + T, dtype=jnp.int32)
        o = diff_attend(q, kk, vv, q_pos, k_pos, lam)
    o = rms_norm(o, subln) * (1.0 - lam_init)
    y = o.reshape(B, T, D_MODEL).astype(u.dtype) @ w_o
    return y, k.reshape(B, T, N_HEADS, HEAD_DIM), v


def run_trunk(x, pool_prefix, n_valid_prefix, past_k, past_v, norm_mix, norm_ffn,
              pool_w, pool_b, pool_scale, w_qkv, q_norm, k_norm,
              lambda_q1, lambda_k1, lambda_q2, lambda_k2, subln, w_o,
              w_gate, w_up, w_down):
    pool_state = k_new = v_new = None
    for i in range(DEPTH):
        u = rms_norm(x, norm_mix[i])
        if i % N_MIXERS == 0:
            y, pool_state = pool_mixer(u, pool_prefix, n_valid_prefix, pool_w, pool_b, pool_scale)
        else:
            lam_init = 0.8 - 0.6 * math.exp(-0.3 * i)
            y, k_new, v_new = diff_attn_mixer(u, past_k, past_v, lam_init, w_qkv, q_norm, k_norm,
                                              lambda_q1, lambda_k1, lambda_q2, lambda_k2, subln, w_o)
        x = x + y.astype(x.dtype)
        x = x + swiglu(x, norm_ffn[i], w_gate[i], w_up[i], w_down[i]).astype(x.dtype)
    return x, pool_state, k_new, v_new


def setup_inputs(seed: int = 0) -> dict:
    key = jax.random.key(seed)
    ks = jax.random.split(key, 24)
    f32 = jnp.float32
    nrm = lambda k, shape, s: jax.random.normal(k, shape, f32) * s
    return {
        "x_prompt": nrm(ks[0], (BATCH, SEQ, D_MODEL), 1.0),
        "x_sample": nrm(ks[1], (DEC_BATCH, DEC_SEQ, D_MODEL), 1.0),
        "state_pool": nrm(ks[2], (DEC_BATCH, POOL_STATE, D_MODEL), 1.0),
        "cache_k": nrm(ks[3], (DEC_BATCH, PAST_LEN, N_HEADS, HEAD_DIM), 1.0),
        "cache_v": nrm(ks[4], (DEC_BATCH, PAST_LEN, N_HEADS, HEAD_DIM), 1.0),
        "norm_mix": 1.0 + nrm(ks[5], (DEPTH, D_MODEL), 0.05),
        "norm_ffn": 1.0 + nrm(ks[6], (DEPTH, D_MODEL), 0.05),
        "pool_w": nrm(ks[7], (N_POOL_GROUPS, POOL_GW, POOL_GW), POOL_GW ** -0.5),
        "pool_b": nrm(ks[8], (D_MODEL,), 0.02),
        "pool_scale": 1.0 + nrm(ks[9], (D_MODEL,), 0.1),
        "w_qkv": nrm(ks[10], (D_MODEL, 3 * D_MODEL), D_MODEL ** -0.5),
        "q_norm": 1.0 + nrm(ks[11], (HEAD_HALF,), 0.05),
        "k_norm": 1.0 + nrm(ks[12], (HEAD_HALF,), 0.05),
        "lambda_q1": nrm(ks[13], (HEAD_HALF,), 0.1),
        "lambda_k1": nrm(ks[14], (HEAD_HALF,), 0.1),
        "lambda_q2": nrm(ks[15], (HEAD_HALF,), 0.1),
        "lambda_k2": nrm(ks[16], (HEAD_HALF,), 0.1),
        "subln": 1.0 + nrm(ks[17], (HEAD_DIM,), 0.05),
        "w_o": nrm(ks[18], (D_MODEL, D_MODEL), D_MODEL ** -0.5),
        "w_gate": nrm(ks[19], (DEPTH, D_MODEL, D_FF), D_MODEL ** -0.5),
        "w_up": nrm(ks[20], (DEPTH, D_MODEL, D_FF), D_MODEL ** -0.5),
        "w_down": nrm(ks[21], (DEPTH, D_FF, D_MODEL), D_FF ** -0.5),
    }


def reference(x_prompt, x_sample, state_pool, cache_k, cache_v, norm_mix, norm_ffn,
              pool_w, pool_b, pool_scale, w_qkv, q_norm, k_norm,
              lambda_q1, lambda_k1, lambda_q2, lambda_k2, subln, w_o,
              w_gate, w_up, w_down):
    weights = (norm_mix, norm_ffn, pool_w, pool_b, pool_scale, w_qkv, q_norm, k_norm,
               lambda_q1, lambda_k1, lambda_q2, lambda_k2, subln, w_o, w_gate, w_up, w_down)
    zero_prefix = jnp.zeros((x_prompt.shape[0], POOL_STATE, D_MODEL), x_prompt.dtype)
    y_prompt, pool_state_prompt, k_prompt, v_prompt = run_trunk(
        x_prompt, zero_prefix, 0, None, None, *weights)
    y_sample, pool_state_sample, k_sample, v_sample = run_trunk(
        x_sample, state_pool, POOL_STATE, cache_k, cache_v, *weights)
    return (y_prompt, y_sample, pool_state_prompt, pool_state_sample,
            k_prompt, v_prompt, k_sample, v_sample)
```

```python
import functools
import math

import jax
import jax.numpy as jnp
from jax import lax
from jax.experimental import pallas as pl
from jax.experimental.pallas import tpu as pltpu

D_MODEL = 2048
CHUNK = 64
POOL_WINDOWS = (2, 4, 8, 16)
POOL_GW = D_MODEL // len(POOL_WINDOWS)
POOL_STATE = max(POOL_WINDOWS) - 1
POOL_PAD = POOL_STATE + 1
N_HEADS = 8
HEAD_HALF = D_MODEL // (2 * N_HEADS)
HEAD_DIM = 2 * HEAD_HALF
EPS = 1e-6
NEG_INF = -1e30
LAM_INIT = 0.8 - 0.6 * math.exp(-0.3 * 1)

V7X_LANES = 128
V7X_VMEM_LIMIT = 60 * 1024 * 1024

F32 = jnp.float32
BF16 = jnp.bfloat16


def _rms(x, g):
    return x * lax.rsqrt(jnp.mean(x * x, axis=-1, keepdims=True) + EPS) * g


def _params(*sem):
    return pltpu.CompilerParams(dimension_semantics=sem, vmem_limit_bytes=V7X_VMEM_LIMIT)


def _pool_kernel(x_ref, pre_ref, g_ref, w_ref, b_ref, sc_ref, x1_ref, st_ref, ext_ref,
                 *, tt, n_valid):
    t = pl.program_id(1)

    @pl.when(t == 0)
    def _():
        ext_ref[0:POOL_PAD, :] = pre_ref[0]

    x = x_ref[0]
    u = _rms(x, g_ref[...])
    ext_ref[POOL_PAD:POOL_PAD + tt, :] = u
    row = t * tt + lax.broadcasted_iota(jnp.int32, (tt, 1), 0)
    for g, w in enumerate(POOL_WINDOWS):
        cols = slice(g * POOL_GW, (g + 1) * POOL_GW)
        ug = u[:, cols]
        s = ug
        for j in range(1, w):
            s = s + ext_ref[POOL_PAD - j:POOL_PAD - j + tt, cols]
        cnt = jnp.minimum(w, row + 1 + n_valid).astype(F32)
        pooled = s / cnt - ug
        mixed = jnp.dot(pooled.astype(BF16), w_ref[g], preferred_element_type=F32)
        x1_ref[0, :, cols] = x[:, cols] + (mixed + b_ref[:, cols]) * sc_ref[:, cols]
    tail = ext_ref[tt:tt + POOL_PAD, :]
    ext_ref[0:POOL_PAD, :] = tail
    st_ref[0] = tail


def _pool_mixer(x, prefix, n_valid, g, w_bf, b, sc, *, tt):
    bsz, t_len, d = x.shape
    tt = min(tt, t_len)
    row = lambda a: a.reshape(1, d)
    vec = pl.BlockSpec((1, d), lambda b_, t_: (0, 0))
    return pl.pallas_call(
        functools.partial(_pool_kernel, tt=tt, n_valid=n_valid),
        grid=(bsz, t_len // tt),
        in_specs=[
            pl.BlockSpec((1, tt, d), lambda b_, t_: (b_, t_, 0)),
            pl.BlockSpec((1, POOL_PAD, d), lambda b_, t_: (b_, 0, 0)),
            vec,
            pl.BlockSpec(w_bf.shape, lambda b_, t_: (0, 0, 0)),
            vec, vec,
        ],
        out_specs=[
            pl.BlockSpec((1, tt, d), lambda b_, t_: (b_, t_, 0)),
            pl.BlockSpec((1, POOL_PAD, d), lambda b_, t_: (b_, 0, 0)),
        ],
        out_shape=[
            jax.ShapeDtypeStruct(x.shape, F32),
            jax.ShapeDtypeStruct((bsz, POOL_PAD, d), F32),
        ],
        scratch_shapes=[pltpu.VMEM((POOL_PAD + tt, d), F32)],
        compiler_params=_params("arbitrary", "arbitrary"),
        name="pool_mixer",
    )(x, prefix, row(g), w_bf, row(b), row(sc))


def _ffn_kernel(x_ref, g_ref, gn_ref, wg_ref, wu_ref, wd_ref, o_ref, *rest, emit_norm):
    if emit_norm:
        un_ref, h_ref = rest
    else:
        (h_ref,) = rest
    j = pl.program_id(1)

    @pl.when(j == 0)
    def _():
        x = x_ref[...]
        h_ref[...] = _rms(x, g_ref[...]).astype(BF16)
        o_ref[...] = x

    h = h_ref[...]
    a = jnp.dot(h, wg_ref[...], preferred_element_type=F32)
    b = jnp.dot(h, wu_ref[...], preferred_element_type=F32)
    act = (jax.nn.silu(a) * b).astype(BF16)
    o_ref[...] += jnp.dot(act, wd_ref[...], preferred_element_type=F32)

    if emit_norm:
        @pl.when(j == pl.num_programs(1) - 1)
        def _():
            un_ref[...] = _rms(o_ref[...], gn_ref[...]).astype(BF16)


def _ffn(x, g, g_next, wg, wu, wd, *, layer, tm, tf, emit_norm):
    n, d = x.shape
    f = wg.shape[2]
    tm = min(tm, n)
    vec = pl.BlockSpec((1, d), lambda i, j: (0, 0))
    tile = pl.BlockSpec((tm, d), lambda i, j: (i, 0))
    out_specs = [tile]
    out_shape = [jax.ShapeDtypeStruct((n, d), F32)]
    if emit_norm:
        out_specs.append(tile)
        out_shape.append(jax.ShapeDtypeStruct((n, d), BF16))
    res = pl.pallas_call(
        functools.partial(_ffn_kernel, emit_norm=emit_norm),
        grid=(n // tm, f // tf),
        in_specs=[
            pl.BlockSpec((tm, d), lambda i, j: (i, 0), pipeline_mode=pl.Buffered(1)),
            vec, vec,
            pl.BlockSpec((None, d, tf), lambda i, j: (layer, 0, j)),
            pl.BlockSpec((None, d, tf), lambda i, j: (layer, 0, j)),
            pl.BlockSpec((None, tf, d), lambda i, j: (layer, j, 0)),
        ],
        out_specs=out_specs,
        out_shape=out_shape,
        scratch_shapes=[pltpu.VMEM((tm, d), BF16)],
        compiler_params=_params("arbitrary", "arbitrary"),
        name="swiglu_ffn",
    )(x, g.reshape(1, d), g_next.reshape(1, d), wg, wu, wd)
    return res if emit_norm else (res[0], None)


def _proj_kernel(u_ref, w_ref, gn_ref, *out_refs, mode):
    y = jnp.dot(u_ref[...], w_ref[...], preferred_element_type=F32)
    tn = y.shape[1]
    if mode == "v":
        out_refs[0][...] = y
        out_refs[1][...] = y.astype(BF16)
        return
    for c in range(tn // HEAD_HALF):
        cols = slice(c * HEAD_HALF, (c + 1) * HEAD_HALF)
        yc = _rms(y[:, cols], gn_ref[...])
        if mode == "q":
            out_refs[0][:, cols] = (yc * (HEAD_HALF ** -0.5)).astype(BF16)
        else:
            out_refs[0][:, cols] = yc
            out_refs[1][:, cols] = yc.astype(BF16)


def _proj(u, w_qkv, gn, *, mode, tm, tn):
    n, d = u.shape
    tm = min(tm, n)
    off = {"q": 0, "k": 1, "v": 2}[mode] * (d // tn)
    tile = pl.BlockSpec((tm, tn), lambda i, j: (i, j))
    if mode == "q":
        out_specs, out_shape = [tile], [jax.ShapeDtypeStruct((n, d), BF16)]
    else:
        out_specs = [tile, tile]
        out_shape = [jax.ShapeDtypeStruct((n, d), F32), jax.ShapeDtypeStruct((n, d), BF16)]
    return pl.pallas_call(
        functools.partial(_proj_kernel, mode=mode),
        grid=(n // tm, d // tn),
        in_specs=[
            pl.BlockSpec((tm, d), lambda i, j: (i, 0)),
            pl.BlockSpec((d, tn), lambda i, j: (0, off + j)),
            pl.BlockSpec((1, HEAD_HALF), lambda i, j: (0, 0)),
        ],
        out_specs=out_specs,
        out_shape=out_shape,
        compiler_params=_params("arbitrary", "arbitrary"),
        name="proj_" + mode,
    )(u, w_qkv, gn.reshape(1, HEAD_HALF))


def _lam(lq1_ref, lk1_ref, lq2_ref, lk2_ref):
    s1 = jnp.sum(lq1_ref[...] * lk1_ref[...], axis=-1, keepdims=True)
    s2 = jnp.sum(lq2_ref[...] * lk2_ref[...], axis=-1, keepdims=True)
    return jnp.exp(s1) - jnp.exp(s2) + LAM_INIT


def _dot_nt(a, b):
    return lax.dot_general(a, b, (((1,), (1,)), ((), ())), preferred_element_type=F32)


def _head_out(o1, o2, lam, subln):
    o = o1 - lam * o2
    return (_rms(o, subln) * (1.0 - LAM_INIT)).astype(BF16)


def _attn_prompt_kernel(slope_ref, lq1_ref, lk1_ref, lq2_ref, lk2_ref, subln_ref,
                        q_ref, k_ref, v_ref, o_ref,
                        caug_ref, bdiag_ref, m_ref, l_ref, acc_ref, *, tq):
    qi = pl.program_id(2)
    slope = slope_ref[0]

    @pl.when(qi == 0)
    def _():
        jl = lax.broadcasted_iota(jnp.int32, (tq, V7X_LANES), 0)
        lane = lax.broadcasted_iota(jnp.int32, (tq, V7X_LANES), 1)
        odd = jl % 2
        col = jnp.where(lane == 0, (jl - odd).astype(F32),
                        jnp.where(lane == 1, odd.astype(F32), 0.0))
        caug_ref[...] = (col * slope).astype(BF16)
        il = lax.broadcasted_iota(jnp.int32, (tq, tq), 0)
        jk = lax.broadcasted_iota(jnp.int32, (tq, tq), 1)
        visible = (jk // CHUNK) <= (il // CHUNK)
        fut = jnp.minimum(il - jk, 0).astype(F32) * (2.0 * slope[:, 0:1])
        bdiag_ref[...] = jnp.where(visible, fut, NEG_INF)

    m_ref[...] = jnp.full(m_ref.shape, NEG_INF, F32)
    l_ref[...] = jnp.zeros(l_ref.shape, F32)
    acc_ref[...] = jnp.zeros(acc_ref.shape, F32)

    q = q_ref[0]
    lane = lax.broadcasted_iota(jnp.int32, (tq, V7X_LANES), 1)
    ones2 = jnp.where(lane < 2, 1.0, 0.0).astype(BF16)
    qa = [jnp.concatenate([q[:, h * HEAD_HALF:(h + 1) * HEAD_HALF], ones2], axis=1)
          for h in range(2)]
    caug = caug_ref[...]

    def block(j, bias, dj):
        start = pl.multiple_of(j * tq, tq)
        k = k_ref[0, pl.ds(start, tq), :]
        v = v_ref[0, pl.ds(start, tq), :]
        for h in range(2):
            ka = jnp.concatenate([k[:, h * HEAD_HALF:(h + 1) * HEAD_HALF], caug], axis=1)
            s = _dot_nt(qa[h], ka)
            if bias is not None:
                s = s + bias
            m_old = m_ref[h]
            m_new = jnp.maximum(m_old, jnp.max(s, axis=-1, keepdims=True) + dj)
            alpha = jnp.exp(m_old - m_new)
            p = jnp.exp(s - (m_new - dj))
            l_ref[h] = alpha * l_ref[h] + jnp.sum(p, axis=-1, keepdims=True)
            acc_ref[h] = alpha * acc_ref[h] + jnp.dot(p.astype(BF16), v,
                                                      preferred_element_type=F32)
            m_ref[h] = m_new

    def body(j, carry):
        dj = slope[:, 0:1] * ((j - qi) * tq).astype(F32)
        block(j, None, dj)
        return carry

    lax.fori_loop(0, qi, body, 0)
    block(qi, bdiag_ref[...], 0.0)

    lam = _lam(lq1_ref, lk1_ref, lq2_ref, lk2_ref)
    o_ref[0] = _head_out(acc_ref[0] / l_ref[0], acc_ref[1] / l_ref[1], lam, subln_ref[...])


def _attn_prompt(q, k, v, slopes, lams, subln, *, tq):
    bsz, t_len, d = q.shape
    tq = min(tq, t_len)
    small = lambda shape: pl.BlockSpec(shape, lambda b, h, i: (0,) * len(shape))
    seq = pl.BlockSpec((1, t_len, HEAD_DIM), lambda b, h, i: (b, 0, h))
    tile = pl.BlockSpec((1, tq, HEAD_DIM), lambda b, h, i: (b, i, h))
    return pl.pallas_call(
        functools.partial(_attn_prompt_kernel, tq=tq),
        grid=(bsz, N_HEADS, t_len // tq),
        in_specs=[
            pl.BlockSpec((1, 1, V7X_LANES), lambda b, h, i: (h, 0, 0)),
            small((1, HEAD_HALF)), small((1, HEAD_HALF)),
            small((1, HEAD_HALF)), small((1, HEAD_HALF)),
            small((1, HEAD_DIM)),
            tile, seq, seq,
        ],
        out_specs=tile,
        out_shape=jax.ShapeDtypeStruct((bsz, t_len, d), BF16),
        scratch_shapes=[
            pltpu.VMEM((tq, V7X_LANES), BF16),
            pltpu.VMEM((tq, tq), F32),
            pltpu.VMEM((2, tq, 1), F32),
            pltpu.VMEM((2, tq, 1), F32),
            pltpu.VMEM((2, tq, HEAD_DIM), F32),
        ],
        compiler_params=_params("arbitrary", "arbitrary", "arbitrary"),
        name="attn_prompt",
    )(slopes, *lams, subln.reshape(1, HEAD_DIM), q, k, v)


def _attn_sample_kernel(slope_ref, lq1_ref, lk1_ref, lq2_ref, lk2_ref, subln_ref,
                        q_ref, kn_ref, vn_ref, ck_ref, cv_ref, o_ref, *, past):
    tq = q_ref.shape[1]
    slope = slope_ref[0][:, 0:1]
    q = q_ref[0]
    kn = kn_ref[0]
    vn = vn_ref[0]
    ck = ck_ref[0].astype(BF16)
    cv = cv_ref[0].astype(BF16)

    q_pos = past + lax.broadcasted_iota(jnp.int32, (tq, 1), 0)

    def logits(qh, kh, k_pos):
        s = _dot_nt(qh, kh)
        dist = jnp.abs(q_pos - k_pos).astype(F32)
        visible = (k_pos // CHUNK) <= (q_pos // CHUNK)
        return jnp.where(visible, s - slope * dist, NEG_INF)

    kp_past = lax.broadcasted_iota(jnp.int32, (1, past), 1)
    kp_new = past + lax.broadcasted_iota(jnp.int32, (1, tq), 1)
    outs = []
    for h in range(2):
        cols = slice(h * HEAD_HALF, (h + 1) * HEAD_HALF)
        sp = logits(q[:, cols], ck[:, cols], kp_past)
        sn = logits(q[:, cols], kn[:, cols], kp_new)
        m = jnp.maximum(jnp.max(sp, axis=-1, keepdims=True),
                        jnp.max(sn, axis=-1, keepdims=True))
        pp = jnp.exp(sp - m)
        pn = jnp.exp(sn - m)
        l = jnp.sum(pp, axis=-1, keepdims=True) + jnp.sum(pn, axis=-1, keepdims=True)
        acc = (jnp.dot(pp.astype(BF16), cv, preferred_element_type=F32)
               + jnp.dot(pn.astype(BF16), vn, preferred_element_type=F32))
        outs.append(acc / l)
    lam = _lam(lq1_ref, lk1_ref, lq2_ref, lk2_ref)
    o_ref[0] = _head_out(outs[0], outs[1], lam, subln_ref[...])


def _attn_sample(q, kn, vn, ck, cv, slopes, lams, subln):
    bsz, tq, d = q.shape
    past = ck.shape[1]
    small = lambda shape: pl.BlockSpec(shape, lambda b, h: (0,) * len(shape))
    new = pl.BlockSpec((1, tq, HEAD_DIM), lambda b, h: (b, 0, h))
    old = pl.BlockSpec((1, past, HEAD_DIM), lambda b, h: (b, 0, h))
    return pl.pallas_call(
        functools.partial(_attn_sample_kernel, past=past),
        grid=(bsz, N_HEADS),
        in_specs=[
            pl.BlockSpec((1, 1, V7X_LANES), lambda b, h: (h, 0, 0)),
            small((1, HEAD_HALF)), small((1, HEAD_HALF)),
            small((1, HEAD_HALF)), small((1, HEAD_HALF)),
            small((1, HEAD_DIM)),
            new, new, new, old, old,
        ],
        out_specs=new,
        out_shape=jax.ShapeDtypeStruct((bsz, tq, d), BF16),
        compiler_params=_params("arbitrary", "arbitrary"),
        name="attn_sample",
    )(slopes, *lams, subln.reshape(1, HEAD_DIM), q, kn, vn, ck, cv)


def _wo_kernel(o_ref, x_ref, w_ref, out_ref):
    out_ref[...] = x_ref[...] + jnp.dot(o_ref[...], w_ref[...], preferred_element_type=F32)


def _wo(o, x, w, *, tm):
    n, d = x.shape
    tm = min(tm, n)
    tile = pl.BlockSpec((tm, d), lambda i: (i, 0))
    return pl.pallas_call(
        _wo_kernel,
        grid=(n // tm,),
        in_specs=[tile, tile, pl.BlockSpec((d, d), lambda i: (0, 0))],
        out_specs=tile,
        out_shape=jax.ShapeDtypeStruct((n, d), F32),
        compiler_params=_params("arbitrary"),
        name="attn_out_proj",
    )(o, x, w)


def _trunk(x, prefix, n_valid, cache, w, *, pool_tt=512, ffn_tm=1024, ffn_tf=512,
           proj_tm=1024, proj_tn=512, attn_tq=512, wo_tm=512):
    bsz, t_len, d = x.shape
    n = bsz * t_len
    x1, state = _pool_mixer(x, prefix, n_valid, w["norm_mix"][0], w["pool_w"], w["pool_b"],
                            w["pool_scale"], tt=pool_tt)
    x2, u2 = _ffn(x1.reshape(n, d), w["norm_ffn"][0], w["norm_mix"][1],
                  w["w_gate"], w["w_up"], w["w_down"],
                  layer=0, tm=ffn_tm, tf=ffn_tf, emit_norm=True)
    (q,) = _proj(u2, w["w_qkv"], w["q_norm"], mode="q", tm=proj_tm, tn=proj_tn)
    k, kb = _proj(u2, w["w_qkv"], w["k_norm"], mode="k", tm=proj_tm, tn=proj_tn)
    v, vb = _proj(u2, w["w_qkv"], w["k_norm"], mode="v", tm=proj_tm, tn=proj_tn)
    seq = lambda a: a.reshape(bsz, t_len, d)
    if cache is None:
        o = _attn_prompt(seq(q), seq(kb), seq(vb), w["slopes"], w["lams"], w["subln"],
                         tq=attn_tq)
    else:
        ck, cv = cache
        past = ck.shape[1]
        o = _attn_sample(seq(q), seq(kb), seq(vb), ck.reshape(bsz, past, d),
                         cv.reshape(bsz, past, d), w["slopes"], w["lams"], w["subln"])
    x3 = _wo(o.reshape(n, d), x2, w["w_o"], tm=wo_tm)
    y, _ = _ffn(x3, w["norm_ffn"][1], w["norm_ffn"][1],
                w["w_gate"], w["w_up"], w["w_down"],
                layer=1, tm=ffn_tm, tf=ffn_tf, emit_norm=False)
    heads = lambda a: a.reshape(bsz, t_len, N_HEADS, HEAD_DIM)
    return y.reshape(bsz, t_len, d), state[:, 1:], heads(k), heads(v)


def kernel(x_prompt, x_sample, state_pool, cache_k, cache_v, norm_mix, norm_ffn, pool_w, pool_b, pool_scale, w_qkv, q_norm, k_norm, lambda_q1, lambda_k1, lambda_q2, lambda_k2, subln, w_o, w_gate, w_up, w_down):
    head = jnp.arange(1, N_HEADS + 1, dtype=F32)
    slopes = 2.0 ** (-8.0 * head / N_HEADS)
    w = dict(
        norm_mix=norm_mix, norm_ffn=norm_ffn, pool_b=pool_b, pool_scale=pool_scale,
        q_norm=q_norm, k_norm=k_norm, subln=subln,
        pool_w=pool_w.astype(BF16), w_qkv=w_qkv.astype(BF16), w_o=w_o.astype(BF16),
        w_gate=w_gate.astype(BF16), w_up=w_up.astype(BF16), w_down=w_down.astype(BF16),
        slopes=jnp.broadcast_to(slopes[:, None, None], (N_HEADS, 1, V7X_LANES)),
        lams=tuple(a.reshape(1, HEAD_HALF) for a in (lambda_q1, lambda_k1, lambda_q2, lambda_k2)),
    )
    zero_prefix = jnp.zeros((x_prompt.shape[0], POOL_PAD, D_MODEL), F32)
    y_p, st_p, k_p, v_p = _trunk(x_prompt, zero_prefix, 0, None, w)
    prefix = jnp.pad(state_pool, ((0, 0), (1, 0), (0, 0)))
    y_s, st_s, k_s, v_s = _trunk(x_sample, prefix, POOL_STATE, (cache_k, cache_v), w)
    return (y_p, y_s, st_p, st_s, k_p, v_p, k_s, v_s)
```

```python
import functools
import math

import jax
import jax.numpy as jnp
from jax import lax
from jax.experimental import pallas as pl
from jax.experimental.pallas import tpu as pltpu

D_MODEL = 2048
CHUNK = 64
POOL_WINDOWS = (2, 4, 8, 16)
POOL_GW = D_MODEL // len(POOL_WINDOWS)
POOL_STATE = max(POOL_WINDOWS) - 1
POOL_PAD = POOL_STATE + 1
N_HEADS = 8
HEAD_HALF = D_MODEL // (2 * N_HEADS)
HEAD_DIM = 2 * HEAD_HALF
EPS = 1e-6
NEG_INF = -1e30
LAM_INIT = 0.8 - 0.6 * math.exp(-0.3 * 1)

ALIBI_SPLIT = 4
ATTN_MAX_BLOCK = ALIBI_SPLIT * 256

V7X_LANES = 128
V7X_VMEM_LIMIT = 60 * 1024 * 1024

F32 = jnp.float32
BF16 = jnp.bfloat16


def _rms(x, g):
    return x * lax.rsqrt(jnp.mean(x * x, axis=-1, keepdims=True) + EPS) * g


def _params(*sem):
    return pltpu.CompilerParams(dimension_semantics=sem, vmem_limit_bytes=V7X_VMEM_LIMIT)


def _pool_kernel(x_ref, pre_ref, g_ref, w_ref, b_ref, sc_ref, x1_ref, st_ref, ext_ref,
                 *, tt, n_valid):
    t = pl.program_id(1)

    @pl.when(t == 0)
    def _():
        ext_ref[0:POOL_PAD, :] = pre_ref[0]

    x = x_ref[0]
    u = _rms(x, g_ref[...])
    ext_ref[POOL_PAD:POOL_PAD + tt, :] = u
    row = t * tt + lax.broadcasted_iota(jnp.int32, (tt, 1), 0)
    for g, w in enumerate(POOL_WINDOWS):
        cols = slice(g * POOL_GW, (g + 1) * POOL_GW)
        ug = u[:, cols]
        s = ug
        for j in range(1, w):
            s = s + ext_ref[POOL_PAD - j:POOL_PAD - j + tt, cols]
        cnt = jnp.minimum(w, row + 1 + n_valid).astype(F32)
        pooled = s / cnt - ug
        mixed = jnp.dot(pooled.astype(BF16), w_ref[g], preferred_element_type=F32)
        x1_ref[0, :, cols] = x[:, cols] + (mixed + b_ref[:, cols]) * sc_ref[:, cols]
    tail = ext_ref[tt:tt + POOL_PAD, :]
    ext_ref[0:POOL_PAD, :] = tail
    st_ref[0] = tail


def _pool_mixer(x, prefix, n_valid, g, w_bf, b, sc, *, tt):
    bsz, t_len, d = x.shape
    tt = min(tt, t_len)
    row = lambda a: a.reshape(1, d)
    vec = pl.BlockSpec((1, d), lambda b_, t_: (0, 0))
    return pl.pallas_call(
        functools.partial(_pool_kernel, tt=tt, n_valid=n_valid),
        grid=(bsz, t_len // tt),
        in_specs=[
            pl.BlockSpec((1, tt, d), lambda b_, t_: (b_, t_, 0)),
            pl.BlockSpec((1, POOL_PAD, d), lambda b_, t_: (b_, 0, 0)),
            vec,
            pl.BlockSpec(w_bf.shape, lambda b_, t_: (0, 0, 0)),
            vec, vec,
        ],
        out_specs=[
            pl.BlockSpec((1, tt, d), lambda b_, t_: (b_, t_, 0)),
            pl.BlockSpec((1, POOL_PAD, d), lambda b_, t_: (b_, 0, 0)),
        ],
        out_shape=[
            jax.ShapeDtypeStruct(x.shape, F32),
            jax.ShapeDtypeStruct((bsz, POOL_PAD, d), F32),
        ],
        scratch_shapes=[pltpu.VMEM((POOL_PAD + tt, d), F32)],
        compiler_params=_params("arbitrary", "arbitrary"),
        name="pool_mixer",
    )(x, prefix, row(g), w_bf, row(b), row(sc))


def _ffn_kernel(x_ref, g_ref, gn_ref, wg_ref, wu_ref, wd_ref, o_ref, *rest, emit_norm):
    if emit_norm:
        un_ref, h_ref = rest
    else:
        (h_ref,) = rest
    j = pl.program_id(1)

    @pl.when(j == 0)
    def _():
        x = x_ref[...]
        h_ref[...] = _rms(x, g_ref[...]).astype(BF16)
        o_ref[...] = x

    h = h_ref[...]
    a = jnp.dot(h, wg_ref[...], preferred_element_type=F32)
    b = jnp.dot(h, wu_ref[...], preferred_element_type=F32)
    act = (jax.nn.silu(a) * b).astype(BF16)
    o_ref[...] += jnp.dot(act, wd_ref[...], preferred_element_type=F32)

    if emit_norm:
        @pl.when(j == pl.num_programs(1) - 1)
        def _():
            un_ref[...] = _rms(o_ref[...], gn_ref[...]).astype(BF16)


def _ffn(x, g, g_next, wg, wu, wd, *, layer, tm, tf, emit_norm):
    n, d = x.shape
    f = wg.shape[2]
    tm = min(tm, n)
    vec = pl.BlockSpec((1, d), lambda i, j: (0, 0))
    tile = pl.BlockSpec((tm, d), lambda i, j: (i, 0))
    out_specs = [tile]
    out_shape = [jax.ShapeDtypeStruct((n, d), F32)]
    if emit_norm:
        out_specs.append(tile)
        out_shape.append(jax.ShapeDtypeStruct((n, d), BF16))
    res = pl.pallas_call(
        functools.partial(_ffn_kernel, emit_norm=emit_norm),
        grid=(n // tm, f // tf),
        in_specs=[
            pl.BlockSpec((tm, d), lambda i, j: (i, 0), pipeline_mode=pl.Buffered(1)),
            vec, vec,
            pl.BlockSpec((None, d, tf), lambda i, j: (layer, 0, j)),
            pl.BlockSpec((None, d, tf), lambda i, j: (layer, 0, j)),
            pl.BlockSpec((None, tf, d), lambda i, j: (layer, j, 0)),
        ],
        out_specs=out_specs,
        out_shape=out_shape,
        scratch_shapes=[pltpu.VMEM((tm, d), BF16)],
        compiler_params=_params("arbitrary", "arbitrary"),
        name="swiglu_ffn",
    )(x, g.reshape(1, d), g_next.reshape(1, d), wg, wu, wd)
    return res if emit_norm else (res[0], None)


def _proj_kernel(u_ref, w_ref, gn_ref, *out_refs, mode):
    y = jnp.dot(u_ref[...], w_ref[...], preferred_element_type=F32)
    tn = y.shape[1]
    if mode == "v":
        out_refs[0][...] = y
        out_refs[1][...] = y.astype(BF16)
        return
    for c in range(tn // HEAD_HALF):
        cols = slice(c * HEAD_HALF, (c + 1) * HEAD_HALF)
        yc = _rms(y[:, cols], gn_ref[...])
        if mode == "q":
            out_refs[0][:, cols] = (yc * (HEAD_HALF ** -0.5)).astype(BF16)
        else:
            out_refs[0][:, cols] = yc
            out_refs[1][:, cols] = yc.astype(BF16)


def _proj(u, w_qkv, gn, *, mode, tm, tn):
    n, d = u.shape
    tm = min(tm, n)
    off = {"q": 0, "k": 1, "v": 2}[mode] * (d // tn)
    tile = pl.BlockSpec((tm, tn), lambda i, j: (i, j))
    if mode == "q":
        out_specs, out_shape = [tile], [jax.ShapeDtypeStruct((n, d), BF16)]
    else:
        out_specs = [tile, tile]
        out_shape = [jax.ShapeDtypeStruct((n, d), F32), jax.ShapeDtypeStruct((n, d), BF16)]
    return pl.pallas_call(
        functools.partial(_proj_kernel, mode=mode),
        grid=(n // tm, d // tn),
        in_specs=[
            pl.BlockSpec((tm, d), lambda i, j: (i, 0)),
            pl.BlockSpec((d, tn), lambda i, j: (0, off + j)),
            pl.BlockSpec((1, HEAD_HALF), lambda i, j: (0, 0)),
        ],
        out_specs=out_specs,
        out_shape=out_shape,
        compiler_params=_params("arbitrary", "arbitrary"),
        name="proj_" + mode,
    )(u, w_qkv, gn.reshape(1, HEAD_HALF))


def _lam(lq1_ref, lk1_ref, lq2_ref, lk2_ref):
    s1 = jnp.sum(lq1_ref[...] * lk1_ref[...], axis=-1, keepdims=True)
    s2 = jnp.sum(lq2_ref[...] * lk2_ref[...], axis=-1, keepdims=True)
    return jnp.exp(s1) - jnp.exp(s2) + LAM_INIT


def _dot_nt(a, b):
    return lax.dot_general(a, b, (((1,), (1,)), ((), ())), preferred_element_type=F32)


def _head_out(o1, o2, lam, subln):
    o = o1 - lam * o2
    return (_rms(o, subln) * (1.0 - LAM_INIT)).astype(BF16)


def _logit_bound(gq, gk):
    peak = lambda g: jnp.max(jnp.abs(g), axis=-1, keepdims=True)
    return (HEAD_HALF ** 0.5) * (1.0 + 2.0 ** -6) * peak(gq) * peak(gk)


def _attn_prompt_kernel(slope_ref, lq1_ref, lk1_ref, lq2_ref, lk2_ref, subln_ref,
                        gq_ref, gk_ref, q_ref, k_ref, v_ref, o_ref,
                        caug_ref, bdiag_ref, m_ref, l_ref, acc_ref, *, tq, fixed_ref):
    qi = pl.program_id(2)
    slope = slope_ref[0]
    n_lane_tiles = tq // V7X_LANES

    @pl.when(qi == 0)
    def _():
        pos = lax.broadcasted_iota(jnp.int32, (tq, V7X_LANES), 0)
        lane = lax.broadcasted_iota(jnp.int32, (tq, V7X_LANES), 1)
        low = pos % ALIBI_SPLIT
        hi_part = (pos - low).astype(F32) * slope
        lo_part = low.astype(F32) * slope
        pick = lambda a, b, c, d: jnp.where(
            lane == 0, a, jnp.where(lane == 1, b, jnp.where(lane == 2, c, jnp.where(
                lane == 3, d, 0.0)))).astype(BF16)
        caug_ref[0] = pick(1.0, 1.0, -hi_part, -lo_part)
        caug_ref[1] = pick(hi_part, lo_part, 1.0, 1.0)
        il = lax.broadcasted_iota(jnp.int32, (tq, tq), 0)
        jk = lax.broadcasted_iota(jnp.int32, (tq, tq), 1)
        visible = (jk // CHUNK) <= (il // CHUNK)
        fut = jnp.minimum(il - jk, 0).astype(F32) * (2.0 * slope[:, 0:1])
        bdiag_ref[...] = jnp.where(visible, fut, NEG_INF)

    if not fixed_ref:
        m_ref[...] = jnp.full(m_ref.shape, NEG_INF, F32)
    l_ref[...] = jnp.zeros(l_ref.shape, F32)
    acc_ref[...] = jnp.zeros(acc_ref.shape, F32)

    q = q_ref[0]
    qa = [jnp.concatenate([q[:, h * HEAD_HALF:(h + 1) * HEAD_HALF], caug_ref[0]], axis=1)
          for h in range(2)]
    caug = caug_ref[1]
    ref = _logit_bound(gq_ref[...], gk_ref[...]) if fixed_ref else 0.0

    def block(j, bias, dj):
        start = pl.multiple_of(j * tq, tq)
        k = k_ref[0, pl.ds(start, tq), :]
        v = v_ref[0, pl.ds(start, tq), :]
        for h in range(2):
            ka = jnp.concatenate([k[:, h * HEAD_HALF:(h + 1) * HEAD_HALF], caug], axis=1)
            s = _dot_nt(qa[h], ka)
            if bias is not None:
                s = s + bias
            if fixed_ref:
                p = jnp.exp(s + dj)
                ps = p[:, 0:V7X_LANES]
                for c in range(1, n_lane_tiles):
                    ps = ps + p[:, c * V7X_LANES:(c + 1) * V7X_LANES]
                l_ref[h] += ps
                acc_ref[h] += jnp.dot(p.astype(BF16), v, preferred_element_type=F32)
            else:
                m_old = m_ref[h]
                m_new = jnp.maximum(m_old, jnp.max(s, axis=-1, keepdims=True) + dj)
                alpha = jnp.exp(m_old - m_new)
                p = jnp.exp(s - (m_new - dj))
                l_ref[h] = alpha * l_ref[h] + jnp.sum(p, axis=-1, keepdims=True)
                acc_ref[h] = alpha * acc_ref[h] + jnp.dot(p.astype(BF16), v,
                                                          preferred_element_type=F32)
                m_ref[h] = m_new

    def past_block(j):
        block(j, None, slope[:, 0:1] * ((j - qi) * tq).astype(F32) - ref)

    def pair(jj, carry):
        past_block(2 * jj)
        past_block(2 * jj + 1)
        return carry

    lax.fori_loop(0, lax.shift_right_logical(qi, 1), pair, 0)

    @pl.when(lax.bitwise_and(qi, 1) == 1)
    def _():
        past_block(qi - 1)

    block(qi, bdiag_ref[...], 0.0 - ref)

    lam = _lam(lq1_ref, lk1_ref, lq2_ref, lk2_ref)
    l = [jnp.sum(l_ref[h], axis=-1, keepdims=True) for h in range(2)]
    o_ref[0] = _head_out(acc_ref[0] / l[0], acc_ref[1] / l[1], lam, subln_ref[...])


def _attn_prompt(q, k, v, slopes, lams, subln, gq, gk, *, tq, fixed_ref):
    bsz, t_len, d = q.shape
    tq = min(tq, t_len)
    assert tq <= ATTN_MAX_BLOCK and tq % CHUNK == 0 and t_len % tq == 0
    small = lambda shape: pl.BlockSpec(shape, lambda b, h, i: (0,) * len(shape))
    seq = pl.BlockSpec((1, t_len, HEAD_DIM), lambda b, h, i: (b, 0, h))
    tile = pl.BlockSpec((1, tq, HEAD_DIM), lambda b, h, i: (b, i, h))
    return pl.pallas_call(
        functools.partial(_attn_prompt_kernel, tq=tq, fixed_ref=fixed_ref),
        grid=(bsz, N_HEADS, t_len // tq),
        in_specs=[
            pl.BlockSpec((1, 1, V7X_LANES), lambda b, h, i: (h, 0, 0)),
            small((1, HEAD_HALF)), small((1, HEAD_HALF)),
            small((1, HEAD_HALF)), small((1, HEAD_HALF)),
            small((1, HEAD_DIM)),
            small((1, HEAD_HALF)), small((1, HEAD_HALF)),
            tile, seq, seq,
        ],
        out_specs=tile,
        out_shape=jax.ShapeDtypeStruct((bsz, t_len, d), BF16),
        scratch_shapes=[
            pltpu.VMEM((2, tq, V7X_LANES), BF16),
            pltpu.VMEM((tq, tq), F32),
            pltpu.VMEM((2, tq, 1), F32),
            pltpu.VMEM((2, tq, V7X_LANES if fixed_ref else 1), F32),
            pltpu.VMEM((2, tq, HEAD_DIM), F32),
        ],
        compiler_params=_params("arbitrary", "arbitrary", "arbitrary"),
        name="attn_prompt_fixed" if fixed_ref else "attn_prompt_online",
    )(slopes, *lams, subln.reshape(1, HEAD_DIM), gq.reshape(1, HEAD_HALF),
      gk.reshape(1, HEAD_HALF), q, k, v)


FIXED_REF_MAX_BOUND = 30.0


def _attn_prompt_guarded(q, k, v, slopes, lams, subln, gq, gk, *, tq):
    run = lambda fixed: functools.partial(_attn_prompt, tq=tq if fixed else tq // 2,
                                          fixed_ref=fixed)
    bound = _logit_bound(gq.reshape(1, HEAD_HALF), gk.reshape(1, HEAD_HALF))[0, 0]
    return lax.cond(bound <= FIXED_REF_MAX_BOUND, run(True), run(False),
                    q, k, v, slopes, lams, subln, gq, gk)


def _attn_sample_kernel(slope_ref, lq1_ref, lk1_ref, lq2_ref, lk2_ref, subln_ref,
                        q_ref, kn_ref, vn_ref, ck_ref, cv_ref, o_ref, *, past):
    tq = q_ref.shape[1]
    slope = slope_ref[0][:, 0:1]
    q = q_ref[0]
    kn = kn_ref[0]
    vn = vn_ref[0]
    ck = ck_ref[0]
    cv = cv_ref[0]

    q_pos = past + lax.broadcasted_iota(jnp.int32, (tq, 1), 0)

    def logits(qh, kh, k_pos):
        s = _dot_nt(qh, kh)
        dist = jnp.abs(q_pos - k_pos).astype(F32)
        visible = (k_pos // CHUNK) <= (q_pos // CHUNK)
        return jnp.where(visible, s - slope * dist, NEG_INF)

    kp_past = lax.broadcasted_iota(jnp.int32, (1, past), 1)
    kp_new = past + lax.broadcasted_iota(jnp.int32, (1, tq), 1)
    outs = []
    for h in range(2):
        cols = slice(h * HEAD_HALF, (h + 1) * HEAD_HALF)
        sp = logits(q[:, cols], ck[:, cols], kp_past)
        sn = logits(q[:, cols], kn[:, cols], kp_new)
        m = jnp.maximum(jnp.max(sp, axis=-1, keepdims=True),
                        jnp.max(sn, axis=-1, keepdims=True))
        pp = jnp.exp(sp - m)
        pn = jnp.exp(sn - m)
        l = jnp.sum(pp, axis=-1, keepdims=True) + jnp.sum(pn, axis=-1, keepdims=True)
        acc = (jnp.dot(pp.astype(BF16), cv, preferred_element_type=F32)
               + jnp.dot(pn.astype(BF16), vn, preferred_element_type=F32))
        outs.append(acc / l)
    lam = _lam(lq1_ref, lk1_ref, lq2_ref, lk2_ref)
    o_ref[0] = _head_out(outs[0], outs[1], lam, subln_ref[...])


def _attn_sample(q, kn, vn, ck, cv, slopes, lams, subln):
    bsz, tq, d = q.shape
    past = ck.shape[1]
    small = lambda shape: pl.BlockSpec(shape, lambda b, h: (0,) * len(shape))
    new = pl.BlockSpec((1, tq, HEAD_DIM), lambda b, h: (b, 0, h))
    old = pl.BlockSpec((1, past, HEAD_DIM), lambda b, h: (b, 0, h))
    return pl.pallas_call(
        functools.partial(_attn_sample_kernel, past=past),
        grid=(bsz, N_HEADS),
        in_specs=[
            pl.BlockSpec((1, 1, V7X_LANES), lambda b, h: (h, 0, 0)),
            small((1, HEAD_HALF)), small((1, HEAD_HALF)),
            small((1, HEAD_HALF)), small((1, HEAD_HALF)),
            small((1, HEAD_DIM)),
            new, new, new, old, old,
        ],
        out_specs=new,
        out_shape=jax.ShapeDtypeStruct((bsz, tq, d), BF16),
        compiler_params=_params("arbitrary", "arbitrary"),
        name="attn_sample",
    )(slopes, *lams, subln.reshape(1, HEAD_DIM), q, kn, vn, ck, cv)


def _wo_kernel(o_ref, x_ref, w_ref, out_ref):
    out_ref[...] = x_ref[...] + jnp.dot(o_ref[...], w_ref[...], preferred_element_type=F32)


def _wo(o, x, w, *, tm):
    n, d = x.shape
    tm = min(tm, n)
    tile = pl.BlockSpec((tm, d), lambda i: (i, 0))
    return pl.pallas_call(
        _wo_kernel,
        grid=(n // tm,),
        in_specs=[tile, tile, pl.BlockSpec((d, d), lambda i: (0, 0))],
        out_specs=tile,
        out_shape=jax.ShapeDtypeStruct((n, d), F32),
        compiler_params=_params("arbitrary"),
        name="attn_out_proj",
    )(o, x, w)


def _trunk(x, prefix, n_valid, cache, w, *, pool_tt=512, ffn_tm=1024, ffn_tf=512,
           proj_tm=1024, proj_tn=512, attn_tq=1024, wo_tm=512):
    bsz, t_len, d = x.shape
    n = bsz * t_len
    x1, state = _pool_mixer(x, prefix, n_valid, w["norm_mix"][0], w["pool_w"], w["pool_b"],
                            w["pool_scale"], tt=pool_tt)
    x2, u2 = _ffn(x1.reshape(n, d), w["norm_ffn"][0], w["norm_mix"][1],
                  w["w_gate"], w["w_up"], w["w_down"],
                  layer=0, tm=ffn_tm, tf=ffn_tf, emit_norm=True)
    (q,) = _proj(u2, w["w_qkv"], w["q_norm"], mode="q", tm=proj_tm, tn=proj_tn)
    k, kb = _proj(u2, w["w_qkv"], w["k_norm"], mode="k", tm=proj_tm, tn=proj_tn)
    v, vb = _proj(u2, w["w_qkv"], w["k_norm"], mode="v", tm=proj_tm, tn=proj_tn)
    seq = lambda a: a.reshape(bsz, t_len, d)
    if cache is None:
        o = _attn_prompt_guarded(seq(q), seq(kb), seq(vb), w["slopes"], w["lams"],
                                 w["subln"], w["q_norm"], w["k_norm"], tq=attn_tq)
    else:
        ck, cv = cache
        past = ck.shape[1]
        flat = lambda c: c.reshape(bsz, past, d).astype(BF16)
        o = _attn_sample(seq(q), seq(kb), seq(vb), flat(ck), flat(cv),
                         w["slopes"], w["lams"], w["subln"])
    k, v, o = lax.optimization_barrier((k, v, o))
    x3 = _wo(o.reshape(n, d), x2, w["w_o"], tm=wo_tm)
    y, _ = _ffn(x3, w["norm_ffn"][1], w["norm_ffn"][1],
                w["w_gate"], w["w_up"], w["w_down"],
                layer=1, tm=ffn_tm, tf=ffn_tf, emit_norm=False)
    heads = lambda a: a.reshape(bsz, t_len, N_HEADS, HEAD_DIM)
    return y.reshape(bsz, t_len, d), state[:, 1:], heads(k), heads(v)


def kernel(x_prompt, x_sample, state_pool, cache_k, cache_v, norm_mix, norm_ffn, pool_w, pool_b, pool_scale, w_qkv, q_norm, k_norm, lambda_q1, lambda_k1, lambda_q2, lambda_k2, subln, w_o, w_gate, w_up, w_down):
    head = jnp.arange(1, N_HEADS + 1, dtype=F32)
    slopes = 2.0 ** (-8.0 * head / N_HEADS)
    w = dict(
        norm_mix=norm_mix, norm_ffn=norm_ffn, pool_b=pool_b, pool_scale=pool_scale,
        q_norm=q_norm, k_norm=k_norm, subln=subln,
        pool_w=pool_w.astype(BF16), w_qkv=w_qkv.astype(BF16), w_o=w_o.astype(BF16),
        w_gate=w_gate.astype(BF16), w_up=w_up.astype(BF16), w_down=w_down.astype(BF16),
        slopes=jnp.broadcast_to(slopes[:, None, None], (N_HEADS, 1, V7X_LANES)),
        lams=tuple(a.reshape(1, HEAD_HALF) for a in (lambda_q1, lambda_k1, lambda_q2, lambda_k2)),
    )
    zero_prefix = jnp.zeros((x_prompt.shape[0], POOL_PAD, D_MODEL), F32)
    y_p, st_p, k_p, v_p = _trunk(x_prompt, zero_prefix, 0, None, w)
    prefix = jnp.pad(state_pool, ((0, 0), (1, 0), (0, 0)))
    y_s, st_s, k_s, v_s = _trunk(x_sample, prefix, POOL_STATE, (cache_k, cache_v), w)
    return (y_p, y_s, st_p, st_s, k_p, v_p, k_s, v_s)
```

```python
import functools
import math

import jax
import jax.numpy as jnp
from jax import lax
from jax.experimental import pallas as pl
from jax.experimental.pallas import tpu as pltpu

D_MODEL = 2048
CHUNK = 64
POOL_WINDOWS = (2, 4, 8, 16)
POOL_GW = D_MODEL // len(POOL_WINDOWS)
POOL_STATE = max(POOL_WINDOWS) - 1
POOL_PAD = POOL_STATE + 1
N_HEADS = 8
HEAD_HALF = D_MODEL // (2 * N_HEADS)
HEAD_DIM = 2 * HEAD_HALF
EPS = 1e-6
NEG_INF = -1e30
LAM_INIT = 0.8 - 0.6 * math.exp(-0.3 * 1)

ALIBI_SPLIT = 4
ATTN_MAX_BLOCK = ALIBI_SPLIT * 256

V7X_LANES = 128
V7X_VMEM_LIMIT = 60 * 1024 * 1024

F32 = jnp.float32
BF16 = jnp.bfloat16


def _rms(x, g):
    return x * lax.rsqrt(jnp.mean(x * x, axis=-1, keepdims=True) + EPS) * g


def _params(*sem):
    return pltpu.CompilerParams(dimension_semantics=sem, vmem_limit_bytes=V7X_VMEM_LIMIT)


def _pool_kernel(x_ref, pre_ref, g_ref, w_ref, b_ref, sc_ref, x1_ref, st_ref, ext_ref,
                 *, tt, n_valid):
    t = pl.program_id(1)

    @pl.when(t == 0)
    def _():
        ext_ref[0:POOL_PAD, :] = pre_ref[0]

    x = x_ref[0]
    u = _rms(x, g_ref[...])
    ext_ref[POOL_PAD:POOL_PAD + tt, :] = u
    row = t * tt + lax.broadcasted_iota(jnp.int32, (tt, 1), 0)
    for g, w in enumerate(POOL_WINDOWS):
        cols = slice(g * POOL_GW, (g + 1) * POOL_GW)
        ug = u[:, cols]
        s = ug
        for j in range(1, w):
            s = s + ext_ref[POOL_PAD - j:POOL_PAD - j + tt, cols]
        cnt = jnp.minimum(w, row + 1 + n_valid).astype(F32)
        pooled = s / cnt - ug
        mixed = jnp.dot(pooled.astype(BF16), w_ref[g], preferred_element_type=F32)
        x1_ref[0, :, cols] = x[:, cols] + (mixed + b_ref[:, cols]) * sc_ref[:, cols]
    tail = ext_ref[tt:tt + POOL_PAD, :]
    ext_ref[0:POOL_PAD, :] = tail
    st_ref[0] = tail


def _pool_mixer(x, prefix, n_valid, g, w_bf, b, sc, *, tt):
    bsz, t_len, d = x.shape
    tt = min(tt, t_len)
    row = lambda a: a.reshape(1, d)
    vec = pl.BlockSpec((1, d), lambda b_, t_: (0, 0))
    return pl.pallas_call(
        functools.partial(_pool_kernel, tt=tt, n_valid=n_valid),
        grid=(bsz, t_len // tt),
        in_specs=[
            pl.BlockSpec((1, tt, d), lambda b_, t_: (b_, t_, 0)),
            pl.BlockSpec((1, POOL_PAD, d), lambda b_, t_: (b_, 0, 0)),
            vec,
            pl.BlockSpec(w_bf.shape, lambda b_, t_: (0, 0, 0)),
            vec, vec,
        ],
        out_specs=[
            pl.BlockSpec((1, tt, d), lambda b_, t_: (b_, t_, 0)),
            pl.BlockSpec((1, POOL_PAD, d), lambda b_, t_: (b_, 0, 0)),
        ],
        out_shape=[
            jax.ShapeDtypeStruct(x.shape, F32),
            jax.ShapeDtypeStruct((bsz, POOL_PAD, d), F32),
        ],
        scratch_shapes=[pltpu.VMEM((POOL_PAD + tt, d), F32)],
        compiler_params=_params("arbitrary", "arbitrary"),
        name="pool_mixer",
    )(x, prefix, row(g), w_bf, row(b), row(sc))


def _ffn_kernel(x_ref, g_ref, gn_ref, wg_ref, wu_ref, wd_ref, o_ref, *rest, emit_norm):
    if emit_norm:
        un_ref, h_ref = rest
    else:
        (h_ref,) = rest
    j = pl.program_id(1)

    @pl.when(j == 0)
    def _():
        x = x_ref[...]
        h_ref[...] = _rms(x, g_ref[...]).astype(BF16)
        o_ref[...] = x

    h = h_ref[...]
    a = jnp.dot(h, wg_ref[...], preferred_element_type=F32)
    b = jnp.dot(h, wu_ref[...], preferred_element_type=F32)
    act = (jax.nn.silu(a) * b).astype(BF16)
    o_ref[...] += jnp.dot(act, wd_ref[...], preferred_element_type=F32)

    if emit_norm:
        @pl.when(j == pl.num_programs(1) - 1)
        def _():
            un_ref[...] = _rms(o_ref[...], gn_ref[...]).astype(BF16)


def _ffn(x, g, g_next, wg, wu, wd, *, layer, tm, tf, emit_norm):
    n, d = x.shape
    f = wg.shape[2]
    tm = min(tm, n)
    vec = pl.BlockSpec((1, d), lambda i, j: (0, 0))
    tile = pl.BlockSpec((tm, d), lambda i, j: (i, 0))
    out_specs = [tile]
    out_shape = [jax.ShapeDtypeStruct((n, d), F32)]
    if emit_norm:
        out_specs.append(tile)
        out_shape.append(jax.ShapeDtypeStruct((n, d), BF16))
    res = pl.pallas_call(
        functools.partial(_ffn_kernel, emit_norm=emit_norm),
        grid=(n // tm, f // tf),
        in_specs=[
            pl.BlockSpec((tm, d), lambda i, j: (i, 0), pipeline_mode=pl.Buffered(1)),
            vec, vec,
            pl.BlockSpec((None, d, tf), lambda i, j: (layer, 0, j)),
            pl.BlockSpec((None, d, tf), lambda i, j: (layer, 0, j)),
            pl.BlockSpec((None, tf, d), lambda i, j: (layer, j, 0)),
        ],
        out_specs=out_specs,
        out_shape=out_shape,
        scratch_shapes=[pltpu.VMEM((tm, d), BF16)],
        compiler_params=_params("arbitrary", "arbitrary"),
        name="swiglu_ffn",
    )(x, g.reshape(1, d), g_next.reshape(1, d), wg, wu, wd)
    return res if emit_norm else (res[0], None)


def _proj_kernel(u_ref, w_ref, gn_ref, *out_refs, mode):
    y = jnp.dot(u_ref[...], w_ref[...], preferred_element_type=F32)
    tn = y.shape[1]
    if mode == "v":
        out_refs[0][...] = y
        out_refs[1][...] = y.astype(BF16)
        return
    for c in range(tn // HEAD_HALF):
        cols = slice(c * HEAD_HALF, (c + 1) * HEAD_HALF)
        yc = _rms(y[:, cols], gn_ref[...])
        if mode == "q":
            out_refs[0][:, cols] = (yc * (HEAD_HALF ** -0.5)).astype(BF16)
        else:
            out_refs[0][:, cols] = yc
            out_refs[1][:, cols] = yc.astype(BF16)


def _proj(u, w_qkv, gn, *, mode, tm, tn):
    n, d = u.shape
    tm = min(tm, n)
    off = {"q": 0, "k": 1, "v": 2}[mode] * (d // tn)
    tile = pl.BlockSpec((tm, tn), lambda i, j: (i, j))
    if mode == "q":
        out_specs, out_shape = [tile], [jax.ShapeDtypeStruct((n, d), BF16)]
    else:
        out_specs = [tile, tile]
        out_shape = [jax.ShapeDtypeStruct((n, d), F32), jax.ShapeDtypeStruct((n, d), BF16)]
    return pl.pallas_call(
        functools.partial(_proj_kernel, mode=mode),
        grid=(n // tm, d // tn),
        in_specs=[
            pl.BlockSpec((tm, d), lambda i, j: (i, 0)),
            pl.BlockSpec((d, tn), lambda i, j: (0, off + j)),
            pl.BlockSpec((1, HEAD_HALF), lambda i, j: (0, 0)),
        ],
        out_specs=out_specs,
        out_shape=out_shape,
        compiler_params=_params("arbitrary", "arbitrary"),
        name="proj_" + mode,
    )(u, w_qkv, gn.reshape(1, HEAD_HALF))


def _lam(lq1_ref, lk1_ref, lq2_ref, lk2_ref):
    s1 = jnp.sum(lq1_ref[...] * lk1_ref[...], axis=-1, keepdims=True)
    s2 = jnp.sum(lq2_ref[...] * lk2_ref[...], axis=-1, keepdims=True)
    return jnp.exp(s1) - jnp.exp(s2) + LAM_INIT


def _dot_nt(a, b):
    return lax.dot_general(a, b, (((1,), (1,)), ((), ())), preferred_element_type=F32)


def _head_out(o1, o2, lam, subln):
    o = o1 - lam * o2
    return (_rms(o, subln) * (1.0 - LAM_INIT)).astype(BF16)


def _logit_bound(gq, gk):
    peak = lambda g: jnp.max(jnp.abs(g), axis=-1, keepdims=True)
    return (HEAD_HALF ** 0.5) * (1.0 + 2.0 ** -6) * peak(gq) * peak(gk)


def _attn_prompt_kernel(slope_ref, lq1_ref, lk1_ref, lq2_ref, lk2_ref, subln_ref,
                        gq_ref, gk_ref, q_ref, k_ref, v_ref, o_ref,
                        caug_ref, bdiag_ref, m_ref, l_ref, acc_ref, *, tq, fixed_ref):
    qi = pl.program_id(2)
    slope = slope_ref[0]

    @pl.when(qi == 0)
    def _():
        pos = lax.broadcasted_iota(jnp.int32, (tq, V7X_LANES), 0)
        lane = lax.broadcasted_iota(jnp.int32, (tq, V7X_LANES), 1)
        low = pos % ALIBI_SPLIT
        hi_part = (pos - low).astype(F32) * slope
        lo_part = low.astype(F32) * slope
        pick = lambda a, b, c, d: jnp.where(
            lane == 0, a, jnp.where(lane == 1, b, jnp.where(lane == 2, c, jnp.where(
                lane == 3, d, 0.0)))).astype(BF16)
        caug_ref[0] = pick(1.0, 1.0, -hi_part, -lo_part)
        caug_ref[1] = pick(hi_part, lo_part, 1.0, 1.0)
        il = lax.broadcasted_iota(jnp.int32, (tq, tq), 0)
        jk = lax.broadcasted_iota(jnp.int32, (tq, tq), 1)
        visible = (jk // CHUNK) <= (il // CHUNK)
        fut = jnp.minimum(il - jk, 0).astype(F32) * (2.0 * slope[:, 0:1])
        bdiag_ref[...] = jnp.where(visible, fut, NEG_INF)

    if not fixed_ref:
        m_ref[...] = jnp.full(m_ref.shape, NEG_INF, F32)
        l_ref[...] = jnp.zeros(l_ref.shape, F32)
        acc_ref[...] = jnp.zeros(acc_ref.shape, F32)

    q = q_ref[0]
    qa = [jnp.concatenate([q[:, h * HEAD_HALF:(h + 1) * HEAD_HALF], caug_ref[0]], axis=1)
          for h in range(2)]
    caug = caug_ref[1]
    ref = _logit_bound(gq_ref[...], gk_ref[...]) if fixed_ref else 0.0

    def block(j, bias, dj, first=False, rows=(0, tq), keys=(0, tq)):
        r0, nr = rows
        c0, nk = keys
        start = pl.multiple_of(j * tq + c0, nk)
        k = k_ref[0, pl.ds(start, nk), :]
        v = v_ref[0, pl.ds(start, nk), :]
        for h in range(2):
            ka = jnp.concatenate([k[:, h * HEAD_HALF:(h + 1) * HEAD_HALF],
                                  caug[c0:c0 + nk]], axis=1)
            s = _dot_nt(qa[h][r0:r0 + nr], ka)
            if bias is not None:
                s = s + bias
            if fixed_ref:
                p = jnp.exp(s + dj)
                ps = p[:, 0:V7X_LANES]
                for c in range(1, nk // V7X_LANES):
                    ps = ps + p[:, c * V7X_LANES:(c + 1) * V7X_LANES]
                pv = jnp.dot(p.astype(BF16), v, preferred_element_type=F32)
                if first:
                    l_ref[h, r0:r0 + nr] = ps
                    acc_ref[h, r0:r0 + nr] = pv
                else:
                    l_ref[h, r0:r0 + nr] += ps
                    acc_ref[h, r0:r0 + nr] += pv
            else:
                assert rows == (0, tq) and keys == (0, tq)
                m_old = m_ref[h]
                m_new = jnp.maximum(m_old, jnp.max(s, axis=-1, keepdims=True) + dj)
                alpha = jnp.exp(m_old - m_new)
                p = jnp.exp(s - (m_new - dj))
                l_ref[h] = alpha * l_ref[h] + jnp.sum(p, axis=-1, keepdims=True)
                acc_ref[h] = alpha * acc_ref[h] + jnp.dot(p.astype(BF16), v,
                                                          preferred_element_type=F32)
                m_ref[h] = m_new

    def past_block(j):
        block(j, None, slope[:, 0:1] * ((j - qi) * tq).astype(F32) - ref)

    def pair(jj, carry):
        past_block(2 * jj)
        past_block(2 * jj + 1)
        return carry

    if fixed_ref:
        half = tq // 2
        assert half % CHUNK == 0 and half % V7X_LANES == 0
        block(qi, bdiag_ref[:, 0:half], 0.0 - ref, first=True, keys=(0, half))
        block(qi, bdiag_ref[half:, half:], 0.0 - ref, rows=(half, half), keys=(half, half))

    lax.fori_loop(0, lax.shift_right_logical(qi, 1), pair, 0)

    @pl.when(lax.bitwise_and(qi, 1) == 1)
    def _():
        past_block(qi - 1)

    if not fixed_ref:
        block(qi, bdiag_ref[...], 0.0)

    lam = _lam(lq1_ref, lk1_ref, lq2_ref, lk2_ref)
    l = [jnp.sum(l_ref[h], axis=-1, keepdims=True) for h in range(2)]
    o_ref[0] = _head_out(acc_ref[0] / l[0], acc_ref[1] / l[1], lam, subln_ref[...])


def _attn_prompt(q, k, v, slopes, lams, subln, gq, gk, *, tq, fixed_ref):
    bsz, t_len, d = q.shape
    tq = min(tq, t_len)
    assert tq <= ATTN_MAX_BLOCK and tq % CHUNK == 0 and t_len % tq == 0
    small = lambda shape: pl.BlockSpec(shape, lambda b, h, i: (0,) * len(shape))
    seq = pl.BlockSpec((1, t_len, HEAD_DIM), lambda b, h, i: (b, 0, h))
    tile = pl.BlockSpec((1, tq, HEAD_DIM), lambda b, h, i: (b, i, h))
    return pl.pallas_call(
        functools.partial(_attn_prompt_kernel, tq=tq, fixed_ref=fixed_ref),
        grid=(bsz, N_HEADS, t_len // tq),
        in_specs=[
            pl.BlockSpec((1, 1, V7X_LANES), lambda b, h, i: (h, 0, 0)),
            small((1, HEAD_HALF)), small((1, HEAD_HALF)),
            small((1, HEAD_HALF)), small((1, HEAD_HALF)),
            small((1, HEAD_DIM)),
            small((1, HEAD_HALF)), small((1, HEAD_HALF)),
            tile, seq, seq,
        ],
        out_specs=tile,
        out_shape=jax.ShapeDtypeStruct((bsz, t_len, d), BF16),
        scratch_shapes=[
            pltpu.VMEM((2, tq, V7X_LANES), BF16),
            pltpu.VMEM((tq, tq), F32),
            pltpu.VMEM((2, tq, 1), F32),
            pltpu.VMEM((2, tq, V7X_LANES if fixed_ref else 1), F32),
            pltpu.VMEM((2, tq, HEAD_DIM), F32),
        ],
        compiler_params=_params("arbitrary", "arbitrary", "arbitrary"),
        name="attn_prompt_fixed" if fixed_ref else "attn_prompt_online",
    )(slopes, *lams, subln.reshape(1, HEAD_DIM), gq.reshape(1, HEAD_HALF),
      gk.reshape(1, HEAD_HALF), q, k, v)


FIXED_REF_MAX_BOUND = 30.0


def _attn_prompt_guarded(q, k, v, slopes, lams, subln, gq, gk, *, tq):
    run = lambda fixed: functools.partial(_attn_prompt, tq=tq if fixed else tq // 2,
                                          fixed_ref=fixed)
    bound = _logit_bound(gq.reshape(1, HEAD_HALF), gk.reshape(1, HEAD_HALF))[0, 0]
    return lax.cond(bound <= FIXED_REF_MAX_BOUND, run(True), run(False),
                    q, k, v, slopes, lams, subln, gq, gk)


def _attn_sample_kernel(slope_ref, lq1_ref, lk1_ref, lq2_ref, lk2_ref, subln_ref,
                        q_ref, kn_ref, vn_ref, ck_ref, cv_ref, o_ref, *, past):
    tq = q_ref.shape[1]
    slope = slope_ref[0][:, 0:1]
    q = q_ref[0]
    kn = kn_ref[0]
    vn = vn_ref[0]
    ck = ck_ref[0].astype(BF16)
    cv = cv_ref[0].astype(BF16)

    q_pos = past + lax.broadcasted_iota(jnp.int32, (tq, 1), 0)

    def logits(qh, kh, k_pos):
        s = _dot_nt(qh, kh)
        dist = jnp.abs(q_pos - k_pos).astype(F32)
        visible = (k_pos // CHUNK) <= (q_pos // CHUNK)
        return jnp.where(visible, s - slope * dist, NEG_INF)

    kp_past = lax.broadcasted_iota(jnp.int32, (1, past), 1)
    kp_new = past + lax.broadcasted_iota(jnp.int32, (1, tq), 1)
    outs = []
    for h in range(2):
        cols = slice(h * HEAD_HALF, (h + 1) * HEAD_HALF)
        sp = logits(q[:, cols], ck[:, cols], kp_past)
        sn = logits(q[:, cols], kn[:, cols], kp_new)
        m = jnp.maximum(jnp.max(sp, axis=-1, keepdims=True),
                        jnp.max(sn, axis=-1, keepdims=True))
        pp = jnp.exp(sp - m)
        pn = jnp.exp(sn - m)
        l = jnp.sum(pp, axis=-1, keepdims=True) + jnp.sum(pn, axis=-1, keepdims=True)
        acc = (jnp.dot(pp.astype(BF16), cv, preferred_element_type=F32)
               + jnp.dot(pn.astype(BF16), vn, preferred_element_type=F32))
        outs.append(acc / l)
    lam = _lam(lq1_ref, lk1_ref, lq2_ref, lk2_ref)
    o_ref[0] = _head_out(outs[0], outs[1], lam, subln_ref[...])


def _attn_sample(q, kn, vn, ck, cv, slopes, lams, subln):
    bsz, tq, d = q.shape
    past = ck.shape[1]
    small = lambda shape: pl.BlockSpec(shape, lambda b, h: (0,) * len(shape))
    new = pl.BlockSpec((1, tq, HEAD_DIM), lambda b, h: (b, 0, h))
    old = pl.BlockSpec((1, past, HEAD_DIM), lambda b, h: (b, 0, h))
    return pl.pallas_call(
        functools.partial(_attn_sample_kernel, past=past),
        grid=(bsz, N_HEADS),
        in_specs=[
            pl.BlockSpec((1, 1, V7X_LANES), lambda b, h: (h, 0, 0)),
            small((1, HEAD_HALF)), small((1, HEAD_HALF)),
            small((1, HEAD_HALF)), small((1, HEAD_HALF)),
            small((1, HEAD_DIM)),
            new, new, new, old, old,
        ],
        out_specs=new,
        out_shape=jax.ShapeDtypeStruct((bsz, tq, d), BF16),
        compiler_params=_params("arbitrary", "arbitrary"),
        name="attn_sample",
    )(slopes, *lams, subln.reshape(1, HEAD_DIM), q, kn, vn, ck, cv)


def _wo_kernel(o_ref, x_ref, w_ref, out_ref):
    out_ref[...] = x_ref[...] + jnp.dot(o_ref[...], w_ref[...], preferred_element_type=F32)


def _wo(o, x, w, *, tm):
    n, d = x.shape
    tm = min(tm, n)
    tile = pl.BlockSpec((tm, d), lambda i: (i, 0))
    return pl.pallas_call(
        _wo_kernel,
        grid=(n // tm,),
        in_specs=[tile, tile, pl.BlockSpec((d, d), lambda i: (0, 0))],
        out_specs=tile,
        out_shape=jax.ShapeDtypeStruct((n, d), F32),
        compiler_params=_params("arbitrary"),
        name="attn_out_proj",
    )(o, x, w)


def _trunk(x, prefix, n_valid, cache, w, *, pool_tt=512, ffn_tm=1024, ffn_tf=512,
           proj_tm=1024, proj_tn=512, attn_tq=1024, wo_tm=512):
    bsz, t_len, d = x.shape
    n = bsz * t_len
    x1, state = _pool_mixer(x, prefix, n_valid, w["norm_mix"][0], w["pool_w"], w["pool_b"],
                            w["pool_scale"], tt=pool_tt)
    x2, u2 = _ffn(x1.reshape(n, d), w["norm_ffn"][0], w["norm_mix"][1],
                  w["w_gate"], w["w_up"], w["w_down"],
                  layer=0, tm=ffn_tm, tf=ffn_tf, emit_norm=True)
    (q,) = _proj(u2, w["w_qkv"], w["q_norm"], mode="q", tm=proj_tm, tn=proj_tn)
    k, kb = _proj(u2, w["w_qkv"], w["k_norm"], mode="k", tm=proj_tm, tn=proj_tn)
    v, vb = _proj(u2, w["w_qkv"], w["k_norm"], mode="v", tm=proj_tm, tn=proj_tn)
    seq = lambda a: a.reshape(bsz, t_len, d)
    if cache is None:
        o = _attn_prompt_guarded(seq(q), seq(kb), seq(vb), w["slopes"], w["lams"],
                                 w["subln"], w["q_norm"], w["k_norm"], tq=attn_tq)
    else:
        ck, cv = cache
        past = ck.shape[1]
        flat = lambda c: c.reshape(bsz, past, d)
        o = _attn_sample(seq(q), seq(kb), seq(vb), flat(ck), flat(cv),
                         w["slopes"], w["lams"], w["subln"])
    k, v, o = lax.optimization_barrier((k, v, o))
    x3 = _wo(o.reshape(n, d), x2, w["w_o"], tm=wo_tm)
    y, _ = _ffn(x3, w["norm_ffn"][1], w["norm_ffn"][1],
                w["w_gate"], w["w_up"], w["w_down"],
                layer=1, tm=ffn_tm, tf=ffn_tf, emit_norm=False)
    heads = lambda a: a.reshape(bsz, t_len, N_HEADS, HEAD_DIM)
    return y.reshape(bsz, t_len, d), state[:, 1:], heads(k), heads(v)


def kernel(x_prompt, x_sample, state_pool, cache_k, cache_v, norm_mix, norm_ffn, pool_w, pool_b, pool_scale, w_qkv, q_norm, k_norm, lambda_q1, lambda_k1, lambda_q2, lambda_k2, subln, w_o, w_gate, w_up, w_down):
    head = jnp.arange(1, N_HEADS + 1, dtype=F32)
    slopes = 2.0 ** (-8.0 * head / N_HEADS)
    w = dict(
        norm_mix=norm_mix, norm_ffn=norm_ffn, pool_b=pool_b, pool_scale=pool_scale,
        q_norm=q_norm, k_norm=k_norm, subln=subln,
        pool_w=pool_w.astype(BF16), w_qkv=w_qkv.astype(BF16), w_o=w_o.astype(BF16),
        w_gate=w_gate.astype(BF16), w_up=w_up.astype(BF16), w_down=w_down.astype(BF16),
        slopes=jnp.broadcast_to(slopes[:, None, None], (N_HEADS, 1, V7X_LANES)),
        lams=tuple(a.reshape(1, HEAD_HALF) for a in (lambda_q1, lambda_k1, lambda_q2, lambda_k2)),
    )
    zero_prefix = jnp.zeros((x_prompt.shape[0], POOL_PAD, D_MODEL), F32)
    y_p, st_p, k_p, v_p = _trunk(x_prompt, zero_prefix, 0, None, w)
    prefix = jnp.pad(state_pool, ((0, 0), (1, 0), (0, 0)))
    y_s, st_s, k_s, v_s = _trunk(x_sample, prefix, POOL_STATE, (cache_k, cache_v), w)
    return (y_p, y_s, st_p, st_s, k_p, v_p, k_s, v_s)
```

```python
import functools
import math

import jax
import jax.numpy as jnp
from jax import lax
from jax.experimental import pallas as pl
from jax.experimental.pallas import tpu as pltpu

D_MODEL = 2048
CHUNK = 64
POOL_WINDOWS = (2, 4, 8, 16)
POOL_GW = D_MODEL // len(POOL_WINDOWS)
POOL_STATE = max(POOL_WINDOWS) - 1
POOL_PAD = POOL_STATE + 1
N_HEADS = 8
HEAD_HALF = D_MODEL // (2 * N_HEADS)
HEAD_DIM = 2 * HEAD_HALF
EPS = 1e-6
NEG_INF = -1e30
LAM_INIT = 0.8 - 0.6 * math.exp(-0.3 * 1)

ALIBI_SPLIT = 4
ATTN_MAX_BLOCK = ALIBI_SPLIT * 256

V7X_LANES = 128
V7X_VMEM_LIMIT = 60 * 1024 * 1024

F32 = jnp.float32
BF16 = jnp.bfloat16


def _rms(x, g):
    return x * lax.rsqrt(jnp.mean(x * x, axis=-1, keepdims=True) + EPS) * g


def _params(*sem):
    return pltpu.CompilerParams(dimension_semantics=sem, vmem_limit_bytes=V7X_VMEM_LIMIT)


def _pool_kernel(x_ref, pre_ref, g_ref, w_ref, b_ref, sc_ref, x1_ref, st_ref, ext_ref,
                 *, tt, n_valid):
    t = pl.program_id(1)

    @pl.when(t == 0)
    def _():
        ext_ref[0:POOL_PAD, :] = pre_ref[0]

    x = x_ref[0]
    u = _rms(x, g_ref[...])
    ext_ref[POOL_PAD:POOL_PAD + tt, :] = u
    row = t * tt + lax.broadcasted_iota(jnp.int32, (tt, 1), 0)
    for g, w in enumerate(POOL_WINDOWS):
        cols = slice(g * POOL_GW, (g + 1) * POOL_GW)
        ug = u[:, cols]
        s = ug
        for j in range(1, w):
            s = s + ext_ref[POOL_PAD - j:POOL_PAD - j + tt, cols]
        cnt = jnp.minimum(w, row + 1 + n_valid).astype(F32)
        pooled = s / cnt - ug
        mixed = jnp.dot(pooled.astype(BF16), w_ref[g], preferred_element_type=F32)
        x1_ref[0, :, cols] = x[:, cols] + (mixed + b_ref[:, cols]) * sc_ref[:, cols]
    tail = ext_ref[tt:tt + POOL_PAD, :]
    ext_ref[0:POOL_PAD, :] = tail
    st_ref[0] = tail


def _pool_mixer(x, prefix, n_valid, g, w_bf, b, sc, *, tt):
    bsz, t_len, d = x.shape
    tt = min(tt, t_len)
    row = lambda a: a.reshape(1, d)
    vec = pl.BlockSpec((1, d), lambda b_, t_: (0, 0))
    return pl.pallas_call(
        functools.partial(_pool_kernel, tt=tt, n_valid=n_valid),
        grid=(bsz, t_len // tt),
        in_specs=[
            pl.BlockSpec((1, tt, d), lambda b_, t_: (b_, t_, 0)),
            pl.BlockSpec((1, POOL_PAD, d), lambda b_, t_: (b_, 0, 0)),
            vec,
            pl.BlockSpec(w_bf.shape, lambda b_, t_: (0, 0, 0)),
            vec, vec,
        ],
        out_specs=[
            pl.BlockSpec((1, tt, d), lambda b_, t_: (b_, t_, 0)),
            pl.BlockSpec((1, POOL_PAD, d), lambda b_, t_: (b_, 0, 0)),
        ],
        out_shape=[
            jax.ShapeDtypeStruct(x.shape, F32),
            jax.ShapeDtypeStruct((bsz, POOL_PAD, d), F32),
        ],
        scratch_shapes=[pltpu.VMEM((POOL_PAD + tt, d), F32)],
        compiler_params=_params("arbitrary", "arbitrary"),
        name="pool_mixer",
    )(x, prefix, row(g), w_bf, row(b), row(sc))


def _ffn_kernel(x_ref, g_ref, gn_ref, wg_ref, wu_ref, wd_ref, o_ref, *rest, emit_norm):
    if emit_norm:
        un_ref, h_ref = rest
    else:
        (h_ref,) = rest
    j = pl.program_id(1)

    @pl.when(j == 0)
    def _():
        x = x_ref[...]
        h_ref[...] = _rms(x, g_ref[...]).astype(BF16)
        o_ref[...] = x

    h = h_ref[...]
    a = jnp.dot(h, wg_ref[...], preferred_element_type=F32)
    b = jnp.dot(h, wu_ref[...], preferred_element_type=F32)
    act = (jax.nn.silu(a) * b).astype(BF16)
    o_ref[...] += jnp.dot(act, wd_ref[...], preferred_element_type=F32)

    if emit_norm:
        @pl.when(j == pl.num_programs(1) - 1)
        def _():
            un_ref[...] = _rms(o_ref[...], gn_ref[...]).astype(BF16)


def _ffn(x, g, g_next, wg, wu, wd, *, layer, tm, tf, emit_norm):
    n, d = x.shape
    f = wg.shape[2]
    tm = min(tm, n)
    vec = pl.BlockSpec((1, d), lambda i, j: (0, 0))
    tile = pl.BlockSpec((tm, d), lambda i, j: (i, 0))
    out_specs = [tile]
    out_shape = [jax.ShapeDtypeStruct((n, d), F32)]
    if emit_norm:
        out_specs.append(tile)
        out_shape.append(jax.ShapeDtypeStruct((n, d), BF16))
    res = pl.pallas_call(
        functools.partial(_ffn_kernel, emit_norm=emit_norm),
        grid=(n // tm, f // tf),
        in_specs=[
            pl.BlockSpec((tm, d), lambda i, j: (i, 0), pipeline_mode=pl.Buffered(1)),
            vec, vec,
            pl.BlockSpec((None, d, tf), lambda i, j: (layer, 0, j)),
            pl.BlockSpec((None, d, tf), lambda i, j: (layer, 0, j)),
            pl.BlockSpec((None, tf, d), lambda i, j: (layer, j, 0)),
        ],
        out_specs=out_specs,
        out_shape=out_shape,
        scratch_shapes=[pltpu.VMEM((tm, d), BF16)],
        compiler_params=_params("arbitrary", "arbitrary"),
        name="swiglu_ffn",
    )(x, g.reshape(1, d), g_next.reshape(1, d), wg, wu, wd)
    return res if emit_norm else (res[0], None)


def _proj_kernel(u_ref, w_ref, gn_ref, *out_refs, mode):
    y = jnp.dot(u_ref[...], w_ref[...], preferred_element_type=F32)
    tn = y.shape[1]
    if mode == "v":
        out_refs[0][...] = y
        out_refs[1][...] = y.astype(BF16)
        return
    for c in range(tn // HEAD_HALF):
        cols = slice(c * HEAD_HALF, (c + 1) * HEAD_HALF)
        yc = _rms(y[:, cols], gn_ref[...])
        if mode == "q":
            out_refs[0][:, cols] = (yc * (HEAD_HALF ** -0.5)).astype(BF16)
        else:
            out_refs[0][:, cols] = yc
            out_refs[1][:, cols] = yc.astype(BF16)


def _proj(u, w_qkv, gn, *, mode, tm, tn):
    n, d = u.shape
    tm = min(tm, n)
    off = {"q": 0, "k": 1, "v": 2}[mode] * (d // tn)
    tile = pl.BlockSpec((tm, tn), lambda i, j: (i, j))
    if mode == "q":
        out_specs, out_shape = [tile], [jax.ShapeDtypeStruct((n, d), BF16)]
    else:
        out_specs = [tile, tile]
        out_shape = [jax.ShapeDtypeStruct((n, d), F32), jax.ShapeDtypeStruct((n, d), BF16)]
    return pl.pallas_call(
        functools.partial(_proj_kernel, mode=mode),
        grid=(n // tm, d // tn),
        in_specs=[
            pl.BlockSpec((tm, d), lambda i, j: (i, 0)),
            pl.BlockSpec((d, tn), lambda i, j: (0, off + j)),
            pl.BlockSpec((1, HEAD_HALF), lambda i, j: (0, 0)),
        ],
        out_specs=out_specs,
        out_shape=out_shape,
        compiler_params=_params("arbitrary", "arbitrary"),
        name="proj_" + mode,
    )(u, w_qkv, gn.reshape(1, HEAD_HALF))


def _lam(lq1_ref, lk1_ref, lq2_ref, lk2_ref):
    s1 = jnp.sum(lq1_ref[...] * lk1_ref[...], axis=-1, keepdims=True)
    s2 = jnp.sum(lq2_ref[...] * lk2_ref[...], axis=-1, keepdims=True)
    return jnp.exp(s1) - jnp.exp(s2) + LAM_INIT


def _dot_nt(a, b):
    return lax.dot_general(a, b, (((1,), (1,)), ((), ())), preferred_element_type=F32)


def _head_out(o1, o2, lam, subln):
    o = o1 - lam * o2
    return (_rms(o, subln) * (1.0 - LAM_INIT)).astype(BF16)


def _logit_bound(gq, gk):
    peak = lambda g: jnp.max(jnp.abs(g), axis=-1, keepdims=True)
    return (HEAD_HALF ** 0.5) * (1.0 + 2.0 ** -6) * peak(gq) * peak(gk)


def _attn_prompt_kernel(slope_ref, lq1_ref, lk1_ref, lq2_ref, lk2_ref, subln_ref,
                        gq_ref, gk_ref, q_ref, k_ref, v_ref, o_ref,
                        caug_ref, bdiag_ref, m_ref, l_ref, acc_ref, *, tq, fixed_ref):
    qi = pl.program_id(2)
    slope = slope_ref[0]

    @pl.when(qi == 0)
    def _():
        pos = lax.broadcasted_iota(jnp.int32, (tq, V7X_LANES), 0)
        lane = lax.broadcasted_iota(jnp.int32, (tq, V7X_LANES), 1)
        low = pos % ALIBI_SPLIT
        hi_part = (pos - low).astype(F32) * slope
        lo_part = low.astype(F32) * slope
        pick = lambda a, b, c, d: jnp.where(
            lane == 0, a, jnp.where(lane == 1, b, jnp.where(lane == 2, c, jnp.where(
                lane == 3, d, 0.0)))).astype(BF16)
        caug_ref[0] = pick(1.0, 1.0, -hi_part, -lo_part)
        caug_ref[1] = pick(hi_part, lo_part, 1.0, 1.0)
        il = lax.broadcasted_iota(jnp.int32, (tq, tq), 0)
        jk = lax.broadcasted_iota(jnp.int32, (tq, tq), 1)
        visible = (jk // CHUNK) <= (il // CHUNK)
        fut = jnp.minimum(il - jk, 0).astype(F32) * (2.0 * slope[:, 0:1])
        bdiag_ref[...] = jnp.where(visible, fut, NEG_INF)

    if not fixed_ref:
        m_ref[...] = jnp.full(m_ref.shape, NEG_INF, F32)
        l_ref[...] = jnp.zeros(l_ref.shape, F32)
        acc_ref[...] = jnp.zeros(acc_ref.shape, F32)

    q = q_ref[0]
    qa = [jnp.concatenate([q[:, h * HEAD_HALF:(h + 1) * HEAD_HALF], caug_ref[0]], axis=1)
          for h in range(2)]
    caug = caug_ref[1]
    ref = _logit_bound(gq_ref[...], gk_ref[...]) if fixed_ref else 0.0

    def block(j, bias, dj, first=False, rows=(0, tq), keys=(0, tq)):
        r0, nr = rows
        c0, nk = keys
        start = pl.multiple_of(j * tq + c0, nk)
        k = k_ref[0, pl.ds(start, nk), :]
        v = v_ref[0, pl.ds(start, nk), :]
        for h in range(2):
            ka = jnp.concatenate([k[:, h * HEAD_HALF:(h + 1) * HEAD_HALF],
                                  caug[c0:c0 + nk]], axis=1)
            s = _dot_nt(qa[h][r0:r0 + nr], ka)
            if bias is not None:
                s = s + bias
            if fixed_ref:
                p = jnp.exp(s + dj)
                ps = p[:, 0:V7X_LANES]
                for c in range(1, nk // V7X_LANES):
                    ps = ps + p[:, c * V7X_LANES:(c + 1) * V7X_LANES]
                pv = jnp.dot(p.astype(BF16), v, preferred_element_type=F32)
                if first:
                    l_ref[h, r0:r0 + nr] = ps
                    acc_ref[h, r0:r0 + nr] = pv
                else:
                    l_ref[h, r0:r0 + nr] += ps
                    acc_ref[h, r0:r0 + nr] += pv
            else:
                assert rows == (0, tq) and keys == (0, tq)
                m_old = m_ref[h]
                m_new = jnp.maximum(m_old, jnp.max(s, axis=-1, keepdims=True) + dj)
                alpha = jnp.exp(m_old - m_new)
                p = jnp.exp(s - (m_new - dj))
                l_ref[h] = alpha * l_ref[h] + jnp.sum(p, axis=-1, keepdims=True)
                acc_ref[h] = alpha * acc_ref[h] + jnp.dot(p.astype(BF16), v,
                                                          preferred_element_type=F32)
                m_ref[h] = m_new

    def past_block(j):
        block(j, None, slope[:, 0:1] * ((j - qi) * tq).astype(F32) - ref)

    def pair(jj, carry):
        past_block(2 * jj)
        past_block(2 * jj + 1)
        return carry

    if fixed_ref:
        half = tq // 2
        assert half % CHUNK == 0 and half % V7X_LANES == 0
        block(qi, bdiag_ref[:, 0:half], 0.0 - ref, first=True, keys=(0, half))
        block(qi, bdiag_ref[half:, half:], 0.0 - ref, rows=(half, half), keys=(half, half))

    lax.fori_loop(0, lax.shift_right_logical(qi, 1), pair, 0)

    @pl.when(lax.bitwise_and(qi, 1) == 1)
    def _():
        past_block(qi - 1)

    if not fixed_ref:
        block(qi, bdiag_ref[...], 0.0)

    lam = _lam(lq1_ref, lk1_ref, lq2_ref, lk2_ref)
    l = [jnp.sum(l_ref[h], axis=-1, keepdims=True) for h in range(2)]
    o_ref[0] = _head_out(acc_ref[0] / l[0], acc_ref[1] / l[1], lam, subln_ref[...])


def _attn_prompt(q, k, v, slopes, lams, subln, gq, gk, *, tq, fixed_ref):
    bsz, t_len, d = q.shape
    tq = min(tq, t_len)
    assert tq <= ATTN_MAX_BLOCK and tq % CHUNK == 0 and t_len % tq == 0
    small = lambda shape: pl.BlockSpec(shape, lambda b, h, i: (0,) * len(shape))
    seq = pl.BlockSpec((1, t_len, HEAD_DIM), lambda b, h, i: (b, 0, h))
    tile = pl.BlockSpec((1, tq, HEAD_DIM), lambda b, h, i: (b, i, h))
    return pl.pallas_call(
        functools.partial(_attn_prompt_kernel, tq=tq, fixed_ref=fixed_ref),
        grid=(bsz, N_HEADS, t_len // tq),
        in_specs=[
            pl.BlockSpec((1, 1, V7X_LANES), lambda b, h, i: (h, 0, 0)),
            small((1, HEAD_HALF)), small((1, HEAD_HALF)),
            small((1, HEAD_HALF)), small((1, HEAD_HALF)),
            small((1, HEAD_DIM)),
            small((1, HEAD_HALF)), small((1, HEAD_HALF)),
            tile, seq, seq,
        ],
        out_specs=tile,
        out_shape=jax.ShapeDtypeStruct((bsz, t_len, d), BF16),
        scratch_shapes=[
            pltpu.VMEM((2, tq, V7X_LANES), BF16),
            pltpu.VMEM((tq, tq), F32),
            pltpu.VMEM((2, tq, 1), F32),
            pltpu.VMEM((2, tq, V7X_LANES if fixed_ref else 1), F32),
            pltpu.VMEM((2, tq, HEAD_DIM), F32),
        ],
        compiler_params=_params("arbitrary", "arbitrary", "arbitrary"),
        name="attn_prompt_fixed" if fixed_ref else "attn_prompt_online",
    )(slopes, *lams, subln.reshape(1, HEAD_DIM), gq.reshape(1, HEAD_HALF),
      gk.reshape(1, HEAD_HALF), q, k, v)


FIXED_REF_MAX_BOUND = 30.0


def _attn_prompt_guarded(q, k, v, slopes, lams, subln, gq, gk, *, tq):
    run = lambda fixed: functools.partial(_attn_prompt, tq=tq if fixed else tq // 2,
                                          fixed_ref=fixed)
    bound = _logit_bound(gq.reshape(1, HEAD_HALF), gk.reshape(1, HEAD_HALF))[0, 0]
    return lax.cond(bound <= FIXED_REF_MAX_BOUND, run(True), run(False),
                    q, k, v, slopes, lams, subln, gq, gk)


def _attn_sample_kernel(slope_ref, lq1_ref, lk1_ref, lq2_ref, lk2_ref, subln_ref,
                        q_ref, kn_ref, vn_ref, ck_ref, cv_ref, o_ref, *, past):
    tq = q_ref.shape[1]
    slope = slope_ref[0][:, 0:1]
    q = q_ref[0]
    kn = kn_ref[0]
    vn = vn_ref[0]
    ck = ck_ref[0].astype(BF16)
    cv = cv_ref[0].astype(BF16)

    q_pos = past + lax.broadcasted_iota(jnp.int32, (tq, 1), 0)

    def logits(qh, kh, k_pos):
        s = _dot_nt(qh, kh)
        dist = jnp.abs(q_pos - k_pos).astype(F32)
        visible = (k_pos // CHUNK) <= (q_pos // CHUNK)
        return jnp.where(visible, s - slope * dist, NEG_INF)

    kp_past = lax.broadcasted_iota(jnp.int32, (1, past), 1)
    kp_new = past + lax.broadcasted_iota(jnp.int32, (1, tq), 1)
    outs = []
    for h in range(2):
        cols = slice(h * HEAD_HALF, (h + 1) * HEAD_HALF)
        sp = logits(q[:, cols], ck[:, cols], kp_past)
        sn = logits(q[:, cols], kn[:, cols], kp_new)
        m = jnp.maximum(jnp.max(sp, axis=-1, keepdims=True),
                        jnp.max(sn, axis=-1, keepdims=True))
        pp = jnp.exp(sp - m)
        pn = jnp.exp(sn - m)
        l = jnp.sum(pp, axis=-1, keepdims=True) + jnp.sum(pn, axis=-1, keepdims=True)
        acc = (jnp.dot(pp.astype(BF16), cv, preferred_element_type=F32)
               + jnp.dot(pn.astype(BF16), vn, preferred_element_type=F32))
        outs.append(acc / l)
    lam = _lam(lq1_ref, lk1_ref, lq2_ref, lk2_ref)
    o_ref[0] = _head_out(outs[0], outs[1], lam, subln_ref[...])


def _attn_sample(q, kn, vn, ck, cv, slopes, lams, subln):
    bsz, tq, d = q.shape
    past = ck.shape[1]
    small = lambda shape: pl.BlockSpec(shape, lambda b, h: (0,) * len(shape))
    new = pl.BlockSpec((1, tq, HEAD_DIM), lambda b, h: (b, 0, h))
    old = pl.BlockSpec((1, past, HEAD_DIM), lambda b, h: (b, 0, h))
    return pl.pallas_call(
        functools.partial(_attn_sample_kernel, past=past),
        grid=(bsz, N_HEADS),
        in_specs=[
            pl.BlockSpec((1, 1, V7X_LANES), lambda b, h: (h, 0, 0)),
            small((1, HEAD_HALF)), small((1, HEAD_HALF)),
            small((1, HEAD_HALF)), small((1, HEAD_HALF)),
            small((1, HEAD_DIM)),
            new, new, new, old, old,
        ],
        out_specs=new,
        out_shape=jax.ShapeDtypeStruct((bsz, tq, d), BF16),
        compiler_params=_params("arbitrary", "arbitrary"),
        name="attn_sample",
    )(slopes, *lams, subln.reshape(1, HEAD_DIM), q, kn, vn, ck, cv)


def _wo_kernel(o_ref, x_ref, w_ref, out_ref):
    out_ref[...] = x_ref[...] + jnp.dot(o_ref[...], w_ref[...], preferred_element_type=F32)


def _wo(o, x, w, *, tm):
    n, d = x.shape
    tm = min(tm, n)
    tile = pl.BlockSpec((tm, d), lambda i: (i, 0))
    return pl.pallas_call(
        _wo_kernel,
        grid=(n // tm,),
        in_specs=[tile, tile, pl.BlockSpec((d, d), lambda i: (0, 0))],
        out_specs=tile,
        out_shape=jax.ShapeDtypeStruct((n, d), F32),
        compiler_params=_params("arbitrary"),
        name="attn_out_proj",
    )(o, x, w)


def _trunk(x, prefix, n_valid, cache, w, *, pool_tt=512, ffn_tm=1024, ffn_tf=512,
           proj_tm=1024, proj_tn=1024, attn_tq=1024, wo_tm=512):
    bsz, t_len, d = x.shape
    n = bsz * t_len
    x1, state = _pool_mixer(x, prefix, n_valid, w["norm_mix"][0], w["pool_w"], w["pool_b"],
                            w["pool_scale"], tt=pool_tt)
    x2, u2 = _ffn(x1.reshape(n, d), w["norm_ffn"][0], w["norm_mix"][1],
                  w["w_gate"], w["w_up"], w["w_down"],
                  layer=0, tm=ffn_tm, tf=ffn_tf, emit_norm=True)
    (q,) = _proj(u2, w["w_qkv"], w["q_norm"], mode="q", tm=proj_tm, tn=proj_tn)
    k, kb = _proj(u2, w["w_qkv"], w["k_norm"], mode="k", tm=proj_tm, tn=proj_tn)
    v, vb = _proj(u2, w["w_qkv"], w["k_norm"], mode="v", tm=proj_tm, tn=proj_tn)
    seq = lambda a: a.reshape(bsz, t_len, d)
    if cache is None:
        o = _attn_prompt_guarded(seq(q), seq(kb), seq(vb), w["slopes"], w["lams"],
                                 w["subln"], w["q_norm"], w["k_norm"], tq=attn_tq)
    else:
        ck, cv = cache
        past = ck.shape[1]
        flat = lambda c: c.reshape(bsz, past, d)
        o = _attn_sample(seq(q), seq(kb), seq(vb), flat(ck), flat(cv),
                         w["slopes"], w["lams"], w["subln"])
    k, v, o = lax.optimization_barrier((k, v, o))
    x3 = _wo(o.reshape(n, d), x2, w["w_o"], tm=wo_tm)
    y, _ = _ffn(x3, w["norm_ffn"][1], w["norm_ffn"][1],
                w["w_gate"], w["w_up"], w["w_down"],
                layer=1, tm=ffn_tm, tf=ffn_tf, emit_norm=False)
    heads = lambda a: a.reshape(bsz, t_len, N_HEADS, HEAD_DIM)
    return y.reshape(bsz, t_len, d), state[:, 1:], heads(k), heads(v)


def kernel(x_prompt, x_sample, state_pool, cache_k, cache_v, norm_mix, norm_ffn, pool_w, pool_b, pool_scale, w_qkv, q_norm, k_norm, lambda_q1, lambda_k1, lambda_q2, lambda_k2, subln, w_o, w_gate, w_up, w_down):
    head = jnp.arange(1, N_HEADS + 1, dtype=F32)
    slopes = 2.0 ** (-8.0 * head / N_HEADS)
    w = dict(
        norm_mix=norm_mix, norm_ffn=norm_ffn, pool_b=pool_b, pool_scale=pool_scale,
        q_norm=q_norm, k_norm=k_norm, subln=subln,
        pool_w=pool_w.astype(BF16), w_qkv=w_qkv.astype(BF16), w_o=w_o.astype(BF16),
        w_gate=w_gate.astype(BF16), w_up=w_up.astype(BF16), w_down=w_down.astype(BF16),
        slopes=jnp.broadcast_to(slopes[:, None, None], (N_HEADS, 1, V7X_LANES)),
        lams=tuple(a.reshape(1, HEAD_HALF) for a in (lambda_q1, lambda_k1, lambda_q2, lambda_k2)),
    )
    zero_prefix = jnp.zeros((x_prompt.shape[0], POOL_PAD, D_MODEL), F32)
    y_p, st_p, k_p, v_p = _trunk(x_prompt, zero_prefix, 0, None, w)
    prefix = jnp.pad(state_pool, ((0, 0), (1, 0), (0, 0)))
    y_s, st_s, k_s, v_s = _trunk(x_sample, prefix, POOL_STATE, (cache_k, cache_v), w)
    return (y_p, y_s, st_p, st_s, k_p, v_p, k_s, v_s)
```
